```python
import jax, jax.numpy as jnp
from jax import lax
import numpy as np

D_MODEL = 1024
BATCH = 32
SEQ = 256
DEPTH = 2
DEC_BATCH = 8
DEC_SEQ = 1024
PAST_LEN = 256

GRID_W = 64
Q_BLOCK = 128
CHUNK = 128
HEAD_DIM = 64
ATTN_W = D_MODEL // 2
N_HEADS = ATTN_W // HEAD_DIM
N_KV_HEADS = N_HEADS // 4
KV_W = N_KV_HEADS * HEAD_DIM
LRU_W = D_MODEL // 4
LRU_BLOCKS = 4
LRU_BW = LRU_W // LRU_BLOCKS
CONV_W = 4
RG_C = 8.0
MLP_W = D_MODEL // 4
MLP_GROUPS = 4
MLP_GW = MLP_W // MLP_GROUPS
MIX_W = ATTN_W + LRU_W + MLP_W
IN_W = ATTN_W + 2 * KV_W + 2 * LRU_W + 2 * MLP_W
SPLITS = (ATTN_W, ATTN_W + KV_W, ATTN_W + 2 * KV_W,
          ATTN_W + 2 * KV_W + LRU_W, ATTN_W + 2 * KV_W + 2 * LRU_W)
D_FF = 4 * D_MODEL
ROPE_THETA = 10000.0
ALPHA = (2 * DEPTH) ** 0.25
BETA = (8 * DEPTH) ** -0.25
EPS = 1e-6

kernel_name = "hybrid_diffusion_parallel_groups_step"


def _layernorm(x, g, b):
    xf = x.astype(jnp.float32)
    mu = jnp.mean(xf, -1, keepdims=True)
    var = jnp.mean(jnp.square(xf - mu), -1, keepdims=True)
    return ((xf - mu) * lax.rsqrt(var + EPS) * g + b).astype(x.dtype)


def _rmsnorm(x, g):
    xf = x.astype(jnp.float32)
    return (xf * lax.rsqrt(jnp.mean(xf * xf, -1, keepdims=True) + EPS) * g).astype(x.dtype)


def _axial_rope(x):
    L = x.shape[1]
    rows = L // GRID_W
    pos_row = jnp.repeat(jnp.arange(rows), GRID_W).astype(jnp.float32)
    pos_col = jnp.tile(jnp.arange(GRID_W), rows).astype(jnp.float32)
    n_f = HEAD_DIM // 4
    inv = ROPE_THETA ** (-jnp.arange(n_f, dtype=jnp.float32) / n_f)
    ang = jnp.concatenate([pos_row[:, None] * inv, pos_col[:, None] * inv], -1)
    cos = jnp.cos(ang)[None, :, None, :]
    sin = jnp.sin(ang)[None, :, None, :]
    x1 = x[..., 0::2].astype(jnp.float32)
    x2 = x[..., 1::2].astype(jnp.float32)
    o = jnp.stack([x1 * cos - x2 * sin, x1 * sin + x2 * cos], -1)
    return o.reshape(x.shape).astype(x.dtype)


def _block_attention(q, k, v):
    B, Lq = q.shape[:2]
    G = N_HEADS // N_KV_HEADS
    nb = Lq // Q_BLOCK
    qb = q.reshape(B, nb, Q_BLOCK, N_KV_HEADS, G, HEAD_DIM).transpose(1, 0, 2, 3, 4, 5)
    scale = HEAD_DIM ** -0.5

    def one_block(qblk):
        s = jnp.einsum('bqkgd,btkd->bkgqt', qblk, k).astype(jnp.float32) * scale
        p = jax.nn.softmax(s, axis=-1).astype(v.dtype)
        return jnp.einsum('bkgqt,btkd->bqkgd', p, v)

    o = lax.map(one_block, qb)
    return o.transpose(1, 0, 2, 3, 4, 5).reshape(B, Lq, ATTN_W)


def _dwconv(x, w, b):
    L = x.shape[1]
    left = (CONV_W - 1) // 2
    right = CONV_W - 1 - left
    xp = jnp.pad(x, ((0, 0), (left, right), (0, 0)))
    return sum(xp[:, j:j + L] * w[j] for j in range(CONV_W)) + b


def _rglru_dir(x, wa, ba, wx, bx, lam, h0):
    B, L, _ = x.shape
    xb = x.reshape(B, L, LRU_BLOCKS, LRU_BW)
    r = jax.nn.sigmoid((jnp.einsum('blnc,ncd->blnd', xb, wa).reshape(B, L, LRU_W) + ba).astype(jnp.float32))
    i = jax.nn.sigmoid((jnp.einsum('blnc,ncd->blnd', xb, wx).reshape(B, L, LRU_W) + bx).astype(jnp.float32))
    log_a = -RG_C * jax.nn.softplus(-lam.astype(jnp.float32)) * r
    a = jnp.exp(log_a)
    u = jnp.sqrt(-jnp.expm1(2.0 * log_a)) * i * x.astype(jnp.float32)

    def combine(e1, e2):
        a1, b1 = e1
        a2, b2 = e2
        return a1 * a2, a2 * b1 + b2

    A, Bc = lax.associative_scan(combine, (a, u), axis=1)
    h = A * h0.astype(jnp.float32)[:, None, :] + Bc
    return h, h[:, -1]


def _rglru_bidir(x, lp, h0):
    hf, sf = _rglru_dir(x, lp['wa'][0], lp['ba'][0], lp['wx'][0], lp['bx'][0], lp['lam'][0], h0[:, 0])
    hb, sb = _rglru_dir(jnp.flip(x, 1), lp['wa'][1], lp['ba'][1], lp['wx'][1], lp['bx'][1],
                        lp['lam'][1], h0[:, 1])
    y = (hf + jnp.flip(hb, 1)).astype(x.dtype)
    return y, jnp.stack([sf, sb], 1).astype(x.dtype)


def _chunk_gmlp(zm, lp):
    z = jax.nn.gelu(zm)
    u, v = z[..., :MLP_W], z[..., MLP_W:]
    v = _layernorm(v, lp['mlp_g'], lp['mlp_b'])
    B, L, _ = v.shape
    vb = v.reshape(B, L // CHUNK, CHUNK, MLP_GROUPS, MLP_GW)
    s = jnp.einsum('gpq,bnqgc->bnpgc', lp['ws'], vb) + lp['bs'].T[None, None, :, :, None]
    return u * s.reshape(B, L, MLP_W)


def _mixer(h, lp, ctx):
    B, L, _ = h.shape
    z = h @ lp['w_in']
    q, k, v, xr, gr, zm = jnp.split(z, SPLITS, axis=-1)
    q = _rmsnorm(q.reshape(B, L, N_HEADS, HEAD_DIM), lp['q_g'])
    k = _rmsnorm(k.reshape(B, L, N_KV_HEADS, HEAD_DIM), lp['k_g'])
    v = v.reshape(B, L, N_KV_HEADS, HEAD_DIM)
    if ctx is None:
        attn = _block_attention(q, k, v)
        h0 = jnp.zeros((B, 2, LRU_W), h.dtype)
    else:
        ck, cv, cs = ctx
        keys = jnp.concatenate([ck, _axial_rope(k)], axis=1)
        vals = jnp.concatenate([cv, v], axis=1)
        attn = _block_attention(_axial_rope(q), keys, vals)
        h0 = cs
    xc = _dwconv(xr, lp['conv_w'], lp['conv_b'])
    y_lru, s_fin = _rglru_bidir(xc, lp, h0)
    y_lru = y_lru * jax.nn.gelu(gr)
    y_mlp = _chunk_gmlp(zm, lp)
    out = jnp.concatenate([attn, y_lru, y_mlp], axis=-1) @ lp['w_out']
    return out, k, v, s_fin


def _layer(x, mod, lp, ctx):
    sh1, sc1, g1, sh2, sc2, g2 = jnp.split(mod, 6, axis=-1)
    out, k, v, s = _mixer(x * (1 + sc1) + sh1, lp, ctx)
    x = _layernorm(ALPHA * x + g1 * out, lp['ln1_g'], lp['ln1_b'])
    hff = x * (1 + sc2) + sh2
    f = jnp.square(jax.nn.relu(hff @ lp['w_ff1'] + lp['b_ff1'])) @ lp['w_ff2'] + lp['b_ff2']
    x = _layernorm(ALPHA * x + g2 * f, lp['ln2_g'], lp['ln2_b'])
    return x, k, v, s


def setup_inputs(seed: int = 0) -> dict:
    key = jax.random.key(seed)
    ks = jax.random.split(key, 40)
    nrm = lambda i, shape, s: jax.random.normal(ks[i], shape, jnp.float32) * s
    a8 = jax.random.uniform(ks[20], (DEPTH, 2, LRU_W), jnp.float32, 0.9, 0.999)
    sig = a8 ** (1.0 / RG_C)
    lam = jnp.log(sig) - jnp.log1p(-sig)
    return {
        "x_prompt": nrm(0, (BATCH, SEQ, D_MODEL), 1.0),
        "x_sample": nrm(1, (DEC_BATCH, DEC_SEQ, D_MODEL), 1.0),
        "c": nrm(2, (DEC_BATCH, D_MODEL), 1.0),
        "cache_k": nrm(3, (DEC_BATCH, DEPTH, PAST_LEN, N_KV_HEADS, HEAD_DIM), 1.0),
        "cache_v": nrm(4, (DEC_BATCH, DEPTH, PAST_LEN, N_KV_HEADS, HEAD_DIM), 1.0),
        "state_lru": nrm(5, (DEC_BATCH, DEPTH, 2, LRU_W), 0.5),
        "c_ctx": nrm(6, (D_MODEL,), 1.0),
        "w_ada": nrm(7, (DEPTH, D_MODEL, 6 * D_MODEL), 0.5 * D_MODEL ** -0.5),
        "b_ada": nrm(8, (DEPTH, 6 * D_MODEL), 0.02),
        "w_in": nrm(9, (DEPTH, D_MODEL, IN_W), D_MODEL ** -0.5),
        "q_norm_g": 1.0 + nrm(10, (DEPTH, HEAD_DIM), 0.02),
        "k_norm_g": 1.0 + nrm(11, (DEPTH, HEAD_DIM), 0.02),
        "conv_w": nrm(12, (DEPTH, CONV_W, LRU_W), CONV_W ** -0.5),
        "conv_b": nrm(13, (DEPTH, LRU_W), 0.02),
        "lru_wa": nrm(14, (DEPTH, 2, LRU_BLOCKS, LRU_BW, LRU_BW), LRU_BW ** -0.5),
        "lru_ba": nrm(15, (DEPTH, 2, LRU_W), 0.02),
        "lru_wx": nrm(16, (DEPTH, 2, LRU_BLOCKS, LRU_BW, LRU_BW), LRU_BW ** -0.5),
        "lru_bx": nrm(17, (DEPTH, 2, LRU_W), 0.02),
        "lru_lam": lam,
        "mlp_norm_g": 1.0 + nrm(18, (DEPTH, MLP_W), 0.02),
        "mlp_norm_b": nrm(19, (DEPTH, MLP_W), 0.02),
        "mlp_ws": nrm(21, (DEPTH, MLP_GROUPS, CHUNK, CHUNK), 0.5 * CHUNK ** -0.5),
        "mlp_bs": 1.0 + nrm(22, (DEPTH, MLP_GROUPS, CHUNK), 0.02),
        "w_out": nrm(23, (DEPTH, MIX_W, D_MODEL), BETA * MIX_W ** -0.5),
        "ln1_g": 1.0 + nrm(24, (DEPTH, D_MODEL), 0.02),
        "ln1_b": nrm(25, (DEPTH, D_MODEL), 0.02),
        "w_ff1": nrm(26, (DEPTH, D_MODEL, D_FF), D_MODEL ** -0.5),
        "b_ff1": nrm(27, (DEPTH, D_FF), 0.02),
        "w_ff2": nrm(28, (DEPTH, D_FF, D_MODEL), BETA * D_FF ** -0.5),
        "b_ff2": nrm(29, (DEPTH, D_MODEL), 0.02),
        "ln2_g": 1.0 + nrm(30, (DEPTH, D_MODEL), 0.02),
        "ln2_b": nrm(31, (DEPTH, D_MODEL), 0.02),
    }


def reference(x_prompt, x_sample, c, cache_k, cache_v, state_lru, c_ctx, w_ada, b_ada, w_in,
              q_norm_g, k_norm_g, conv_w, conv_b, lru_wa, lru_ba, lru_wx, lru_bx, lru_lam,
              mlp_norm_g, mlp_norm_b, mlp_ws, mlp_bs, w_out, ln1_g, ln1_b, w_ff1, b_ff1,
              w_ff2, b_ff2, ln2_g, ln2_b):
    y_prompt = x_prompt
    y_sample = x_sample
    new_k, new_v, new_s = [], [], []
    for l in range(DEPTH):
        lp = dict(w_in=w_in[l], q_g=q_norm_g[l], k_g=k_norm_g[l], conv_w=conv_w[l], conv_b=conv_b[l],
                  wa=lru_wa[l], ba=lru_ba[l], wx=lru_wx[l], bx=lru_bx[l], lam=lru_lam[l],
                  mlp_g=mlp_norm_g[l], mlp_b=mlp_norm_b[l], ws=mlp_ws[l], bs=mlp_bs[l],
                  w_out=w_out[l], ln1_g=ln1_g[l], ln1_b=ln1_b[l], w_ff1=w_ff1[l], b_ff1=b_ff1[l],
                  w_ff2=w_ff2[l], b_ff2=b_ff2[l], ln2_g=ln2_g[l], ln2_b=ln2_b[l])
        mod_ctx = (jax.nn.silu(c_ctx) @ w_ada[l] + b_ada[l])[None, None, :]
        mod_lat = (jax.nn.silu(c) @ w_ada[l] + b_ada[l])[:, None, :]
        y_prompt, k_l, v_l, s_l = _layer(y_prompt, mod_ctx, lp, None)
        new_k.append(k_l)
        new_v.append(v_l)
        new_s.append(s_l)
        y_sample, _, _, _ = _layer(y_sample, mod_lat, lp,
                                   (cache_k[:, l], cache_v[:, l], state_lru[:, l]))
    new_cache_k = jnp.stack(new_k, axis=1)
    new_cache_v = jnp.stack(new_v, axis=1)
    new_state_lru = jnp.stack(new_s, axis=1)
    return (y_prompt, y_sample, new_cache_k, new_cache_v, new_state_lru)
```

```python
import functools

import jax
import jax.numpy as jnp
from jax import lax
from jax.experimental import pallas as pl
from jax.experimental.pallas import tpu as pltpu

D_MODEL = 1024
DEPTH = 2
GRID_W = 64
CHUNK = 128
HEAD_DIM = 64
ATTN_W = 512
N_HEADS = 8
N_KV_HEADS = 2
KV_W = 128
LRU_W = 256
LRU_BLOCKS = 4
LRU_BW = 64
CONV_W = 4
RG_C = 8.0
MLP_W = 256
MLP_GROUPS = 4
MLP_GW = 64
MIX_W = 1024
IN_W = 1792
D_FF = 4096
ROPE_THETA = 10000.0
ALPHA = (2 * DEPTH) ** 0.25
EPS = 1e-6

Q0, K0, V0, XR0, GR0, ZM0 = 0, 512, 640, 768, 1024, 1280
QK_W = ATTN_W + KV_W
LANES = 128
Q_BLOCK = 256
FF_CHUNK = 1024
VMEM_LIMIT = 56 * 1024 * 1024

F32 = jnp.float32
BF16 = jnp.bfloat16


def _layernorm(x, g, b):
    mu = jnp.mean(x, -1, keepdims=True)
    xc = x - mu
    var = jnp.mean(xc * xc, -1, keepdims=True)
    return xc * lax.rsqrt(var + EPS) * g + b


def _gelu(x):
    return jax.nn.gelu(x, approximate=True)


def _resident(shape):
    n = len(shape)
    return pl.BlockSpec(shape, lambda *_: (0,) * n, pipeline_mode=pl.Buffered(1))


def _mod_kernel(cond_ref, w_ref, b_ref, o_ref):
    cond = cond_ref[...]
    o_ref[...] = jnp.dot(jax.nn.silu(cond), w_ref[...], preferred_element_type=F32,
                         precision=lax.Precision.HIGHEST) + b_ref[...]


def _modulation(cond, w_ada, b_ada):
    rows = cond.shape[0]
    tn = 1536
    return pl.pallas_call(
        _mod_kernel,
        grid=(DEPTH, 6 * D_MODEL // tn),
        in_specs=[pl.BlockSpec((rows, D_MODEL), lambda l, j: (0, 0)),
                  pl.BlockSpec((None, D_MODEL, tn), lambda l, j: (l, 0, j)),
                  pl.BlockSpec((None, 1, tn), lambda l, j: (l, 0, j))],
        out_specs=pl.BlockSpec((None, rows, tn), lambda l, j: (l, 0, j)),
        out_shape=jax.ShapeDtypeStruct((DEPTH, rows, 6 * D_MODEL), F32),
        compiler_params=pltpu.CompilerParams(dimension_semantics=("arbitrary", "arbitrary"),
                                             vmem_limit_bytes=VMEM_LIMIT),
        name="adaln_modulation",
    )(cond, w_ada, b_ada.reshape(DEPTH, 1, 6 * D_MODEL))


def _scan(a, u, length, reverse):
    row = lax.broadcasted_iota(jnp.int32, a.shape, 0)
    d = 1
    while d < length:
        if reverse:
            keep = row < length - d
            a_s = pltpu.roll(a, length - d, 0)
            u_s = pltpu.roll(u, length - d, 0)
        else:
            keep = row >= d
            a_s = pltpu.roll(a, d, 0)
            u_s = pltpu.roll(u, d, 0)
        u = a * jnp.where(keep, u_s, 0.0) + u
        a = a * jnp.where(keep, a_s, 1.0)
        d *= 2
    return a, u


def _mixer_kernel(*refs, seq_len, n_seq, latent):
    it = iter(refs)
    x_ref, mod_ref, w_in_ref, hsum_ref, gqk_ref = next(it), next(it), next(it), next(it), next(it)
    if latent:
        cos_ref, sin_a_ref, sin_b_ref = next(it), next(it), next(it)
        ck_ref, cv_ref, st_ref = next(it), next(it), next(it)
    conv_w_ref, conv_b_ref, wg_ref, bg_ref, lam_ref = next(it), next(it), next(it), next(it), next(it)
    mlp_g_ref, mlp_b_ref, ws_ref, bsb_ref = next(it), next(it), next(it), next(it)
    w_out_ref, ln_g_ref, ln_b_ref = next(it), next(it), next(it)
    x1_ref = next(it)
    if not latent:
        k_out_ref, v_out_ref, s_out_ref = next(it), next(it), next(it)
    z_s, q_s, kt_s, vp_s, mix_s = next(it), next(it), next(it), next(it), next(it)

    L = seq_len
    past = ck_ref.shape[0] if latent else 0
    T = past + L

    mod = mod_ref[...]
    sh1, sc1, g1 = mod[:, 0:D_MODEL], mod[:, D_MODEL:2 * D_MODEL], mod[:, 2 * D_MODEL:3 * D_MODEL]
    h = (x_ref[...] * (1.0 + sc1) + sh1).astype(BF16)
    z_s[...] = jnp.dot(h, w_in_ref[...], preferred_element_type=F32)

    qk = z_s[:, Q0:Q0 + QK_W]
    ssq = jnp.dot((qk * qk).astype(BF16), hsum_ref[...], preferred_element_type=F32)
    qkn = qk * lax.rsqrt(ssq * (1.0 / HEAD_DIM) + EPS) * gqk_ref[...]
    if not latent:
        k_out_ref[...] = qkn[:, ATTN_W:]
        v_out_ref[...] = z_s[:, V0:V0 + KV_W]

    lane = lax.broadcasted_iota(jnp.int32, (1, LANES), 1)
    lo_half = lane < HEAD_DIM

    for s in range(n_seq):
        r0 = s * L
        rows = slice(r0, r0 + L)
        qkn_s = qkn[rows]
        k_cur = qkn_s[:, ATTN_W:]
        v_cur = z_s[rows, V0:V0 + KV_W]
        if latent:
            cos, sin_a, sin_b = cos_ref[...], sin_a_ref[...], sin_b_ref[...]

            def rope(t):
                return (t * cos + pltpu.roll(t, LANES - 1, 1) * sin_a + pltpu.roll(t, 1, 1) * sin_b)

            for g in range(ATTN_W // LANES):
                qg = rope(qkn_s[:, g * LANES:(g + 1) * LANES]) * (HEAD_DIM ** -0.5)
                q_s[rows, g * LANES:(g + 1) * LANES] = qg.astype(BF16)
            k_all = jnp.concatenate([ck_ref[...], rope(k_cur)], axis=0)
            v_all = jnp.concatenate([cv_ref[...], v_cur], axis=0)
        else:
            q_s[rows, :] = (qkn_s[:, :ATTN_W] * (HEAD_DIM ** -0.5)).astype(BF16)
            k_all, v_all = k_cur, v_cur

        kt = k_all.T
        zer = jnp.zeros((HEAD_DIM, T), F32)
        v_rot = pltpu.roll(v_all, HEAD_DIM, 1)
        for j in range(N_KV_HEADS):
            kj = kt[j * HEAD_DIM:(j + 1) * HEAD_DIM]
            kt_s[2 * j] = jnp.concatenate([kj, zer], axis=0).astype(BF16)
            kt_s[2 * j + 1] = jnp.concatenate([zer, kj], axis=0).astype(BF16)
            v_even = v_all if j == 0 else v_rot
            v_odd = v_rot if j == 0 else v_all
            vp_s[2 * j] = jnp.where(lo_half, v_even, 0.0).astype(BF16)
            vp_s[2 * j + 1] = jnp.where(lo_half, 0.0, v_odd).astype(BF16)

        qb = min(Q_BLOCK, L)
        for g in range(ATTN_W // LANES):
            j = g // (ATTN_W // LANES // N_KV_HEADS)

            def q_step(i, carry, g=g, j=j):
                qrows = pl.ds(pl.multiple_of(r0 + i * qb, qb), qb)
                qg = q_s[qrows, g * LANES:(g + 1) * LANES]
                acc = jnp.zeros((qb, LANES), F32)
                for par in range(2):
                    sc = jnp.dot(qg, kt_s[2 * j + par], preferred_element_type=F32)
                    e = jnp.exp(sc - jnp.max(sc, -1, keepdims=True))
                    o = jnp.dot(e.astype(BF16), vp_s[2 * j + par], preferred_element_type=F32)
                    acc = acc + o / jnp.sum(e, -1, keepdims=True)
                mix_s[qrows, g * LANES:(g + 1) * LANES] = acc.astype(BF16)
                return carry

            lax.fori_loop(0, L // qb, q_step, 0)

        xr = z_s[rows, XR0:XR0 + LRU_W]
        row = lax.broadcasted_iota(jnp.int32, (L, LRU_W), 0)
        cw = conv_w_ref[...]
        xc = (jnp.where(row >= 1, pltpu.roll(xr, 1, 0), 0.0) * cw[0:1]
              + xr * cw[1:2]
              + jnp.where(row < L - 1, pltpu.roll(xr, L - 1, 0), 0.0) * cw[2:3]
              + jnp.where(row < L - 2, pltpu.roll(xr, L - 2, 0), 0.0) * cw[3:4]
              + conv_b_ref[...])
        xcb = xc.astype(BF16)
        lam = lam_ref[...]
        neg = -lam
        decay = -RG_C * (jnp.maximum(neg, 0.0) + jnp.log1p(jnp.exp(-jnp.abs(neg))))
        y = None
        for d in range(2):
            gates = jnp.dot(xcb, wg_ref[:, 2 * d * LRU_W:2 * (d + 1) * LRU_W],
                            preferred_element_type=F32) + bg_ref[:, 2 * d * LRU_W:2 * (d + 1) * LRU_W]
            r = jax.nn.sigmoid(gates[:, :LRU_W])
            i_g = jax.nn.sigmoid(gates[:, LRU_W:])
            log_a = decay[d:d + 1] * r
            a = jnp.exp(log_a)
            th = jnp.tanh(log_a)
            u = jnp.sqrt(-2.0 * th / (1.0 - th)) * i_g * xc
            a_c, b_c = _scan(a, u, L, reverse=(d == 1))
            hd = a_c * st_ref[d:d + 1, :] + b_c if latent else b_c
            if not latent:
                fin = hd[L - 1:L] if d == 0 else hd[0:1]
                s_out_ref[s, d:d + 1, :] = fin
            y = hd if y is None else y + hd
        y = y * _gelu(z_s[rows, GR0:GR0 + LRU_W])
        mix_s[rows, ATTN_W:ATTN_W + LRU_W] = y.astype(BF16)

        glane = lax.broadcasted_iota(jnp.int32, (CHUNK, MLP_W), 1) // MLP_GW
        for c in range(L // CHUNK):
            crow = slice(r0 + c * CHUNK, r0 + (c + 1) * CHUNK)
            zg = _gelu(z_s[crow, ZM0:ZM0 + 2 * MLP_W])
            u_m = zg[:, :MLP_W]
            vn = _layernorm(zg[:, MLP_W:], mlp_g_ref[...], mlp_b_ref[...]).astype(BF16)
            full = jnp.dot(ws_ref[...], vn, preferred_element_type=F32)
            sm = bsb_ref[...]
            for gi in range(MLP_GROUPS):
                sm = sm + jnp.where(glane == gi, full[gi * CHUNK:(gi + 1) * CHUNK], 0.0)
            mix_s[crow, ATTN_W + LRU_W:] = (u_m * sm).astype(BF16)

    out = jnp.dot(mix_s[...], w_out_ref[...], preferred_element_type=F32)
    x1_ref[...] = _layernorm(ALPHA * x_ref[...] + g1 * out, ln_g_ref[...], ln_b_ref[...])


def _mixer(x, mod, p, *, seq_len, n_seq, latent, ctx=None, rope=None):
    B, L, _ = x.shape
    M = n_seq * L
    n_steps = B // n_seq
    x2 = x.reshape(B * L, D_MODEL)
    T = L + (ctx[0].shape[1] if latent else 0)

    in_specs = [pl.BlockSpec((M, D_MODEL), lambda i: (i, 0))]
    args = [x2]
    if latent:
        in_specs.append(pl.BlockSpec((None, 1, 6 * D_MODEL), lambda i: (i, 0, 0)))
    else:
        ctx_row = mod.shape[0] - 1
        in_specs.append(pl.BlockSpec((None, 1, 6 * D_MODEL), lambda i: (ctx_row, 0, 0)))
    args.append(mod)
    for a in (p["w_in"], p["hsum"], p["gqk"]):
        in_specs.append(_resident(a.shape))
        args.append(a)
    if latent:
        for a in rope:
            in_specs.append(_resident(a.shape))
            args.append(a)
        ck, cv, st = ctx
        past = ck.shape[1]
        in_specs += [pl.BlockSpec((None, past, KV_W), lambda i: (i, 0, 0)),
                     pl.BlockSpec((None, past, KV_W), lambda i: (i, 0, 0)),
                     pl.BlockSpec((None, 2, LRU_W), lambda i: (i, 0, 0))]
        args += [ck, cv, st]
    for name in ("conv_w", "conv_b", "wg", "bg", "lam", "mlp_g", "mlp_b", "ws", "bsb", "w_out", "ln1_g", "ln1_b"):
        a = p[name]
        in_specs.append(_resident(a.shape))
        args.append(a)

    out_shape = [jax.ShapeDtypeStruct((B * L, D_MODEL), F32)]
    out_specs = [pl.BlockSpec((M, D_MODEL), lambda i: (i, 0))]
    if not latent:
        out_shape += [jax.ShapeDtypeStruct((B * L, KV_W), F32), jax.ShapeDtypeStruct((B * L, KV_W), F32),
                      jax.ShapeDtypeStruct((B, 2, LRU_W), F32)]
        out_specs += [pl.BlockSpec((M, KV_W), lambda i: (i, 0)), pl.BlockSpec((M, KV_W), lambda i: (i, 0)),
                      pl.BlockSpec((n_seq, 2, LRU_W), lambda i: (i, 0, 0))]

    scratch = [pltpu.VMEM((M, IN_W), F32),
               pltpu.VMEM((M, ATTN_W), BF16),
               pltpu.VMEM((2 * N_KV_HEADS, LANES, T), BF16),
               pltpu.VMEM((2 * N_KV_HEADS, T, LANES), BF16),
               pltpu.VMEM((M, MIX_W), BF16)]

    outs = pl.pallas_call(
        functools.partial(_mixer_kernel, seq_len=L, n_seq=n_seq, latent=latent),
        grid=(n_steps,),
        in_specs=in_specs,
        out_specs=out_specs,
        out_shape=out_shape,
        scratch_shapes=scratch,
        compiler_params=pltpu.CompilerParams(dimension_semantics=("arbitrary",), vmem_limit_bytes=VMEM_LIMIT),
        name="mixer_latent" if latent else "mixer_context",
    )(*args)
    return outs


def _ffn_kernel(x_ref, mod_ref, w1_ref, b1_ref, w2_ref, b2_ref, g_ref, b_ref, o_ref):
    x = x_ref[...]
    mod = mod_ref[...]
    sh2, sc2, g2 = mod[:, 3 * D_MODEL:4 * D_MODEL], mod[:, 4 * D_MODEL:5 * D_MODEL], mod[:, 5 * D_MODEL:]
    h = (x * (1.0 + sc2) + sh2).astype(BF16)
    acc = jnp.zeros(x.shape, F32)
    for c in range(D_FF // FF_CHUNK):
        cols = slice(c * FF_CHUNK, (c + 1) * FF_CHUNK)
        t = jnp.dot(h, w1_ref[:, cols], preferred_element_type=F32) + b1_ref[:, cols]
        t = jnp.square(jnp.maximum(t, 0.0)).astype(BF16)
        acc = acc + jnp.dot(t, w2_ref[cols, :], preferred_element_type=F32)
    f = acc + b2_ref[...]
    o_ref[...] = _layernorm(ALPHA * x + g2 * f, g_ref[...], b_ref[...])


def _ffn(x2, mod, p, *, seq_len, latent, tm):
    rows = x2.shape[0]
    if latent:
        mod_map = lambda i: ((i * tm) // seq_len, 0, 0)
    else:
        ctx_row = mod.shape[0] - 1
        mod_map = lambda i: (ctx_row, 0, 0)
    in_specs = [pl.BlockSpec((tm, D_MODEL), lambda i: (i, 0)),
                pl.BlockSpec((None, 1, 6 * D_MODEL), mod_map)]
    args = [x2, mod]
    for name in ("w_ff1", "b_ff1", "w_ff2", "b_ff2", "ln2_g", "ln2_b"):
        in_specs.append(_resident(p[name].shape))
        args.append(p[name])
    return pl.pallas_call(
        _ffn_kernel,
        grid=(rows // tm,),
        in_specs=in_specs,
        out_specs=pl.BlockSpec((tm, D_MODEL), lambda i: (i, 0)),
        out_shape=jax.ShapeDtypeStruct((rows, D_MODEL), F32),
        compiler_params=pltpu.CompilerParams(dimension_semantics=("arbitrary",), vmem_limit_bytes=VMEM_LIMIT),
        name="ffn_latent" if latent else "ffn_context",
    )(*args)


def _rope_tables(length):
    rows = length // GRID_W
    pos_row = jnp.repeat(jnp.arange(rows), GRID_W).astype(F32)
    pos_col = jnp.tile(jnp.arange(GRID_W), rows).astype(F32)
    n_f = HEAD_DIM // 4
    inv = ROPE_THETA ** (-jnp.arange(n_f, dtype=F32) / n_f)
    ang = jnp.concatenate([pos_row[:, None] * inv, pos_col[:, None] * inv], -1)
    cos = jnp.repeat(jnp.cos(ang), 2, axis=-1)
    sin = jnp.repeat(jnp.sin(ang), 2, axis=-1)
    even = (jnp.arange(HEAD_DIM) % 2 == 0)[None, :]
    sin_a = jnp.where(even, -sin, 0.0)
    sin_b = jnp.where(even, 0.0, sin)
    rep = LANES // HEAD_DIM
    return tuple(jnp.tile(t, (1, rep)).astype(F32) for t in (cos, sin_a, sin_b))


def _block_diag(w):
    out = jnp.zeros((LRU_W, LRU_W), w.dtype)
    for n in range(LRU_BLOCKS):
        out = out.at[n * LRU_BW:(n + 1) * LRU_BW, n * LRU_BW:(n + 1) * LRU_BW].set(w[n])
    return out


def kernel(x_prompt, x_sample, c, cache_k, cache_v, state_lru, c_ctx, w_ada, b_ada, w_in, q_norm_g, k_norm_g,
           conv_w, conv_b, lru_wa, lru_ba, lru_wx, lru_bx, lru_lam, mlp_norm_g, mlp_norm_b, mlp_ws, mlp_bs,
           w_out, ln1_g, ln1_b, w_ff1, b_ff1, w_ff2, b_ff2, ln2_g, ln2_b):
    batch, seq, _ = x_prompt.shape
    dec_batch, dec_seq, _ = x_sample.shape
    past = cache_k.shape[2]

    n_cond = dec_batch + 1
    cond_rows = -(-n_cond // 8) * 8
    cond = jnp.zeros((cond_rows, D_MODEL), F32).at[:dec_batch].set(c).at[dec_batch].set(c_ctx)
    mod_all = _modulation(cond, w_ada, b_ada)[:, :n_cond]

    head_id = jnp.arange(QK_W) // HEAD_DIM
    hsum = (head_id[:, None] == head_id[None, :]).astype(BF16)
    rope = _rope_tables(dec_seq)
    ck_all = cache_k.reshape(dec_batch, DEPTH, past, KV_W)
    cv_all = cache_v.reshape(dec_batch, DEPTH, past, KV_W)

    y_prompt, y_sample = x_prompt, x_sample
    new_k, new_v, new_s = [], [], []
    for l in range(DEPTH):
        p = dict(
            w_in=w_in[l].astype(BF16), hsum=hsum,
            gqk=jnp.concatenate([jnp.tile(q_norm_g[l], N_HEADS), jnp.tile(k_norm_g[l], N_KV_HEADS)])[None, :],
            conv_w=conv_w[l], conv_b=conv_b[l][None, :],
            wg=jnp.concatenate([_block_diag(lru_wa[l, 0]), _block_diag(lru_wx[l, 0]),
                                _block_diag(lru_wa[l, 1]), _block_diag(lru_wx[l, 1])], axis=1).astype(BF16),
            bg=jnp.concatenate([lru_ba[l, 0], lru_bx[l, 0], lru_ba[l, 1], lru_bx[l, 1]])[None, :],
            lam=lru_lam[l],
            mlp_g=mlp_norm_g[l][None, :], mlp_b=mlp_norm_b[l][None, :],
            ws=mlp_ws[l].reshape(MLP_GROUPS * CHUNK, CHUNK).astype(BF16),
            bsb=jnp.repeat(mlp_bs[l].T, MLP_GW, axis=1),
            w_out=w_out[l].astype(BF16), ln1_g=ln1_g[l][None, :], ln1_b=ln1_b[l][None, :],
            w_ff1=w_ff1[l].astype(BF16), b_ff1=b_ff1[l][None, :], w_ff2=w_ff2[l].astype(BF16),
            b_ff2=b_ff2[l][None, :], ln2_g=ln2_g[l][None, :], ln2_b=ln2_b[l][None, :],
        )
        mod = mod_all[l][:, None, :]

        x1, k_l, v_l, s_l = _mixer(y_prompt, mod, p, seq_len=seq, n_seq=2, latent=False)
        y_prompt = _ffn(x1, mod, p, seq_len=seq, latent=False, tm=512).reshape(batch, seq, D_MODEL)
        new_k.append(k_l.reshape(batch, seq, N_KV_HEADS, HEAD_DIM))
        new_v.append(v_l.reshape(batch, seq, N_KV_HEADS, HEAD_DIM))
        new_s.append(s_l)

        (x1,) = _mixer(y_sample, mod, p, seq_len=dec_seq, n_seq=1, latent=True,
                       ctx=(ck_all[:, l], cv_all[:, l], state_lru[:, l]), rope=rope)
        y_sample = _ffn(x1, mod, p, seq_len=dec_seq, latent=True, tm=512).reshape(dec_batch, dec_seq, D_MODEL)

    return (y_prompt, y_sample, jnp.stack(new_k, axis=1), jnp.stack(new_v, axis=1), jnp.stack(new_s, axis=1))
```

```python
import functools
import math

import jax
import jax.numpy as jnp
from jax import lax
from jax.experimental import pallas as pl
from jax.experimental.pallas import tpu as pltpu

D_MODEL = 1024
DEPTH = 2
GRID_W = 64
CHUNK = 128
HEAD_DIM = 64
ATTN_W = 512
N_HEADS = 8
N_KV_HEADS = 2
KV_W = 128
LRU_W = 256
LRU_BLOCKS = 4
LRU_BW = 64
CONV_W = 4
RG_C = 8.0
MLP_W = 256
MLP_GROUPS = 4
MLP_GW = 64
MIX_W = 1024
IN_W = 1792
D_FF = 4096
ROPE_THETA = 10000.0
ALPHA = (2 * DEPTH) ** 0.25
EPS = 1e-6

Q0, K0, V0, XR0, GR0, ZM0 = 0, 512, 640, 768, 1024, 1280
QK_W = ATTN_W + KV_W
LANES = 128
SUBLANES = 8
N_QGROUPS = ATTN_W // LANES
Q_BLOCK = 256
FF_CHUNK = 1024
VMEM_LIMIT = 56 * 1024 * 1024
Q_SCALE = HEAD_DIM ** -0.5 * math.log2(math.e)

F32 = jnp.float32
BF16 = jnp.bfloat16


def _layernorm(x, g, b):
    mu = jnp.mean(x, -1, keepdims=True)
    xc = x - mu
    var = jnp.mean(xc * xc, -1, keepdims=True)
    return xc * lax.rsqrt(var + EPS) * g + b


def _gelu(x):
    return jax.nn.gelu(x, approximate=True)


def _layer_spec(arr, layer):
    n = arr.ndim - 1
    return pl.BlockSpec((None,) + arr.shape[1:], lambda *_: (layer,) + (0,) * n, pipeline_mode=pl.Buffered(1))


def _const_spec(arr):
    n = arr.ndim
    return pl.BlockSpec(arr.shape, lambda *_: (0,) * n, pipeline_mode=pl.Buffered(1))


def _mod_kernel(cond_ref, w_ref, b_ref, o_ref):
    cond = cond_ref[...]
    o_ref[...] = jnp.dot(jax.nn.silu(cond), w_ref[...], preferred_element_type=F32,
                         precision=lax.Precision.HIGHEST) + b_ref[...]


def _modulation(cond, w_ada, b_ada):
    rows = cond.shape[0]
    tn = 1536
    return pl.pallas_call(
        _mod_kernel,
        grid=(DEPTH, 6 * D_MODEL // tn),
        in_specs=[pl.BlockSpec((rows, D_MODEL), lambda l, j: (0, 0)),
                  pl.BlockSpec((None, D_MODEL, tn), lambda l, j: (l, 0, j)),
                  pl.BlockSpec((None, 1, tn), lambda l, j: (l, 0, j))],
        out_specs=pl.BlockSpec((None, rows, tn), lambda l, j: (l, 0, j)),
        out_shape=jax.ShapeDtypeStruct((DEPTH, rows, 6 * D_MODEL), F32),
        compiler_params=pltpu.CompilerParams(dimension_semantics=("arbitrary", "arbitrary"),
                                             vmem_limit_bytes=VMEM_LIMIT),
        name="adaln_modulation",
    )(cond, w_ada, b_ada.reshape(DEPTH, 1, 6 * D_MODEL))


def _tile_scan(a, u, reverse):
    n_t, w = a.shape[0] // SUBLANES, a.shape[1]
    a3 = a.reshape(n_t, SUBLANES, w)
    u3 = u.reshape(n_t, SUBLANES, w)
    sub = lax.broadcasted_iota(jnp.int32, (1, SUBLANES, w), 1)
    d = 1
    while d < SUBLANES:
        keep = (sub < SUBLANES - d) if reverse else (sub >= d)
        shift = SUBLANES - d if reverse else d
        a_sh = pltpu.roll(a3, shift, 1)
        u_sh = pltpu.roll(u3, shift, 1)
        u3 = a3 * jnp.where(keep, u_sh, 0.0) + u3
        a3 = a3 * jnp.where(keep, a_sh, 1.0)
        d *= 2
    return a3.reshape(n_t * SUBLANES, w), u3.reshape(n_t * SUBLANES, w)


def _mixer_kernel(*refs, seq_len, n_seq, latent):
    it = iter(refs)
    x_ref, mod_ref, w_in_ref, hsum_ref, gqk_ref = next(it), next(it), next(it), next(it), next(it)
    if latent:
        cos_ref, sin_a_ref, sin_b_ref = next(it), next(it), next(it)
        ck_ref, cv_ref, st_ref = next(it), next(it), next(it)
    conv_w_ref, conv_b_ref, wg_ref, bg_ref, lam_ref = next(it), next(it), next(it), next(it), next(it)
    mlp_g_ref, mlp_b_ref, ws_ref, bsb_ref = next(it), next(it), next(it), next(it)
    w_out_ref, ln_g_ref, ln_b_ref = next(it), next(it), next(it)
    if not latent:
        next(it), next(it), next(it)
    x1_ref = next(it)
    if not latent:
        k_out_ref, v_out_ref, s_out_ref = next(it), next(it), next(it)
    z_s, q_s, kt_s, vp_s, mix_s = next(it), next(it), next(it), next(it), next(it)
    af_s, uf_s, ab_s, ub_s = next(it), next(it), next(it), next(it)

    L = seq_len
    past = ck_ref.shape[0] if latent else 0
    T = past + L
    n_tiles = L // SUBLANES

    mod = mod_ref[...]
    sh1, sc1, g1 = mod[:, 0:D_MODEL], mod[:, D_MODEL:2 * D_MODEL], mod[:, 2 * D_MODEL:3 * D_MODEL]
    h = (x_ref[...] * (1.0 + sc1) + sh1).astype(BF16)
    z_s[...] = jnp.dot(h, w_in_ref[...], preferred_element_type=F32)

    qk = z_s[:, Q0:Q0 + QK_W]
    ssq = jnp.dot((qk * qk).astype(BF16), hsum_ref[...], preferred_element_type=F32)
    qkn = qk * lax.rsqrt(ssq * (1.0 / HEAD_DIM) + EPS) * gqk_ref[...]

    lane = lax.broadcasted_iota(jnp.int32, (1, LANES), 1)
    lo_half = lane < HEAD_DIM

    lam = lam_ref[...]
    neg = -lam
    decay = -RG_C * (jnp.maximum(neg, 0.0) + jnp.log1p(jnp.exp(-jnp.abs(neg))))

    for s in range(n_seq):
        r0 = s * L
        rows = slice(r0, r0 + L)
        qkn_s = qkn[rows]
        k_cur = qkn_s[:, ATTN_W:]
        v_cur = z_s[rows, V0:V0 + KV_W]
        if latent:
            cos, sin_a, sin_b = cos_ref[...], sin_a_ref[...], sin_b_ref[...]

            def rope(t):
                return (t * cos + pltpu.roll(t, LANES - 1, 1) * sin_a + pltpu.roll(t, 1, 1) * sin_b)

            for g in range(N_QGROUPS):
                qg = rope(qkn_s[:, g * LANES:(g + 1) * LANES]) * Q_SCALE
                q_s[rows, g * LANES:(g + 1) * LANES] = qg.astype(BF16)
            k_all = jnp.concatenate([ck_ref[...], rope(k_cur)], axis=0)
            v_all = jnp.concatenate([cv_ref[...], v_cur], axis=0)
        else:
            k_out_ref[s] = k_cur
            v_out_ref[s] = v_cur
            q_s[rows, :] = (qkn_s[:, :ATTN_W] * Q_SCALE).astype(BF16)
            k_all, v_all = k_cur, v_cur

        kt = k_all.T
        zer = jnp.zeros((HEAD_DIM, T), F32)
        v_rot = pltpu.roll(v_all, HEAD_DIM, 1)
        for j in range(N_KV_HEADS):
            kj = kt[j * HEAD_DIM:(j + 1) * HEAD_DIM]
            kt_s[2 * j] = jnp.concatenate([kj, zer], axis=0).astype(BF16)
            kt_s[2 * j + 1] = jnp.concatenate([zer, kj], axis=0).astype(BF16)
            v_even = v_all if j == 0 else v_rot
            v_odd = v_rot if j == 0 else v_all
            vp_s[2 * j] = jnp.where(lo_half, v_even, 0.0).astype(BF16)
            vp_s[2 * j + 1] = jnp.where(lo_half, 0.0, v_odd).astype(BF16)

        qb = min(Q_BLOCK, L)
        groups_per_kv = N_QGROUPS // N_KV_HEADS
        for j in range(N_KV_HEADS):

            def q_step(i, carry, j=j):
                qrows = pl.ds(pl.multiple_of(r0 + i * qb, qb), qb)
                for g in range(j * groups_per_kv, (j + 1) * groups_per_kv):
                    qg = q_s[qrows, g * LANES:(g + 1) * LANES]
                    acc = None
                    for par in range(2):
                        sc = jnp.dot(qg, kt_s[2 * j + par], preferred_element_type=F32)
                        e = jnp.exp2(sc - jnp.max(sc, -1, keepdims=True))
                        o = jnp.dot(e.astype(BF16), vp_s[2 * j + par], preferred_element_type=F32)
                        o = o / jnp.sum(e, -1, keepdims=True)
                        acc = o if acc is None else acc + o
                    mix_s[qrows, g * LANES:(g + 1) * LANES] = acc.astype(BF16)
                return carry

            lax.fori_loop(0, L // qb, q_step, 0)

        xr = z_s[rows, XR0:XR0 + LRU_W]
        row = lax.broadcasted_iota(jnp.int32, (L, LRU_W), 0)
        cw = conv_w_ref[...]
        xc = (jnp.where(row >= 1, pltpu.roll(xr, 1, 0), 0.0) * cw[0:1]
              + xr * cw[1:2]
              + jnp.where(row < L - 1, pltpu.roll(xr, L - 1, 0), 0.0) * cw[2:3]
              + jnp.where(row < L - 2, pltpu.roll(xr, L - 2, 0), 0.0) * cw[3:4]
              + conv_b_ref[...])
        xcb = xc.astype(BF16)
        for d, (a_s, u_s) in enumerate(((af_s, uf_s), (ab_s, ub_s))):
            gates = jnp.dot(xcb, wg_ref[:, 2 * d * LRU_W:2 * (d + 1) * LRU_W],
                            preferred_element_type=F32) + bg_ref[:, 2 * d * LRU_W:2 * (d + 1) * LRU_W]
            r = jax.nn.sigmoid(gates[:, :LRU_W])
            i_g = jax.nn.sigmoid(gates[:, LRU_W:])
            log_a = decay[d:d + 1] * r
            a = jnp.exp(log_a)
            th = jnp.tanh(log_a)
            u = jnp.sqrt(-2.0 * th / (1.0 - th)) * i_g * xc
            a_t, u_t = _tile_scan(a, u, reverse=(d == 1))
            a_s[rows, :] = a_t
            u_s[rows, :] = u_t

        def tile_step(t, carry):
            cf, cb = carry
            rf = pl.ds(pl.multiple_of(r0 + t * SUBLANES, SUBLANES), SUBLANES)
            rb = pl.ds(pl.multiple_of(r0 + (n_tiles - 1 - t) * SUBLANES, SUBLANES), SUBLANES)
            hf = af_s[rf, :] * cf + uf_s[rf, :]
            hb = ab_s[rb, :] * cb + ub_s[rb, :]
            uf_s[rf, :] = hf
            ub_s[rb, :] = hb
            return (jnp.broadcast_to(hf[SUBLANES - 1:SUBLANES], (SUBLANES, LRU_W)),
                    jnp.broadcast_to(hb[0:1], (SUBLANES, LRU_W)))

        if latent:
            init = (jnp.broadcast_to(st_ref[0:1, :], (SUBLANES, LRU_W)),
                    jnp.broadcast_to(st_ref[1:2, :], (SUBLANES, LRU_W)))
        else:
            init = (jnp.zeros((SUBLANES, LRU_W), F32), jnp.zeros((SUBLANES, LRU_W), F32))
        fin_f, fin_b = lax.fori_loop(0, n_tiles, tile_step, init, unroll=4)
        if not latent:
            s_out_ref[s, 0:1, :] = fin_f[0:1]
            s_out_ref[s, 1:2, :] = fin_b[0:1]
        y = (uf_s[rows, :] + ub_s[rows, :]) * _gelu(z_s[rows, GR0:GR0 + LRU_W])
        mix_s[rows, ATTN_W:ATTN_W + LRU_W] = y.astype(BF16)

        glane = lax.broadcasted_iota(jnp.int32, (CHUNK, MLP_W), 1) // MLP_GW
        for c in range(L // CHUNK):
            crow = slice(r0 + c * CHUNK, r0 + (c + 1) * CHUNK)
            zg = _gelu(z_s[crow, ZM0:ZM0 + 2 * MLP_W])
            u_m = zg[:, :MLP_W]
            vn = _layernorm(zg[:, MLP_W:], mlp_g_ref[...], mlp_b_ref[...]).astype(BF16)
            full = jnp.dot(ws_ref[...], vn, preferred_element_type=F32)
            sm = bsb_ref[...]
            for gi in range(MLP_GROUPS):
                sm = sm + jnp.where(glane == gi, full[gi * CHUNK:(gi + 1) * CHUNK], 0.0)
            mix_s[crow, ATTN_W + LRU_W:] = (u_m * sm).astype(BF16)

    out = jnp.dot(mix_s[...], w_out_ref[...], preferred_element_type=F32)
    x1_ref[...] = _layernorm(ALPHA * x_ref[...] + g1 * out, ln_g_ref[...], ln_b_ref[...])


def _mixer(x2, mod, p, layer, *, batch, seq_len, n_seq, latent, ctx=None, rope=None, caches=None):
    B, L = batch, seq_len
    M = n_seq * L
    n_steps = B // n_seq
    T = L + (ctx[0].shape[2] if latent else 0)

    mod_row = (lambda i: i) if latent else (lambda i: mod.shape[1] - 1)
    in_specs = [pl.BlockSpec((M, D_MODEL), lambda i: (i, 0)),
                pl.BlockSpec((None, None, 1, 6 * D_MODEL), lambda i: (layer, mod_row(i), 0, 0))]
    args = [x2, mod]
    for name in ("w_in", "hsum", "gqk"):
        a = p[name]
        in_specs.append(_const_spec(a) if name == "hsum" else _layer_spec(a, layer))
        args.append(a)
    if latent:
        for a in rope:
            in_specs.append(_const_spec(a))
            args.append(a)
        ck, cv, st = ctx
        past = ck.shape[2]
        in_specs += [pl.BlockSpec((None, None, past, KV_W), lambda i: (i, layer, 0, 0)),
                     pl.BlockSpec((None, None, past, KV_W), lambda i: (i, layer, 0, 0)),
                     pl.BlockSpec((None, None, 2, LRU_W), lambda i: (i, layer, 0, 0))]
        args += [ck, cv, st]
    for name in ("conv_w", "conv_b", "wg", "bg", "lam", "mlp_g", "mlp_b", "ws", "bsb", "w_out", "ln1_g", "ln1_b"):
        in_specs.append(_layer_spec(p[name], layer))
        args.append(p[name])

    out_shape = [jax.ShapeDtypeStruct((B * L, D_MODEL), F32)]
    out_specs = [pl.BlockSpec((M, D_MODEL), lambda i: (i, 0))]
    aliases = {}
    if not latent:
        if caches is None:
            caches = (jnp.zeros((B, DEPTH, L, KV_W), F32), jnp.zeros((B, DEPTH, L, KV_W), F32),
                      jnp.zeros((B, DEPTH, 2, LRU_W), F32))
        for c in caches:
            aliases[len(args)] = len(out_shape)
            in_specs.append(pl.BlockSpec(memory_space=pl.ANY))
            args.append(c)
            out_shape.append(jax.ShapeDtypeStruct(c.shape, F32))
        out_specs += [pl.BlockSpec((n_seq, None, L, KV_W), lambda i: (i, layer, 0, 0)),
                      pl.BlockSpec((n_seq, None, L, KV_W), lambda i: (i, layer, 0, 0)),
                      pl.BlockSpec((n_seq, None, 2, LRU_W), lambda i: (i, layer, 0, 0))]

    scratch = [pltpu.VMEM((M, IN_W), F32),
               pltpu.VMEM((M, ATTN_W), BF16),
               pltpu.VMEM((2 * N_KV_HEADS, LANES, T), BF16),
               pltpu.VMEM((2 * N_KV_HEADS, T, LANES), BF16),
               pltpu.VMEM((M, MIX_W), BF16)]
    scratch += [pltpu.VMEM((M, LRU_W), F32)] * 4

    return pl.pallas_call(
        functools.partial(_mixer_kernel, seq_len=L, n_seq=n_seq, latent=latent),
        grid=(n_steps,),
        in_specs=in_specs,
        out_specs=out_specs,
        out_shape=out_shape,
        scratch_shapes=scratch,
        input_output_aliases=aliases,
        compiler_params=pltpu.CompilerParams(dimension_semantics=("arbitrary",), vmem_limit_bytes=VMEM_LIMIT),
        name="mixer_latent" if latent else "mixer_context",
    )(*args)


def _ffn_kernel(x_ref, mod_ref, w1_ref, b1_ref, w2_ref, b2_ref, g_ref, b_ref, o_ref):
    x = x_ref[...]
    mod = mod_ref[...]
    sh2, sc2, g2 = mod[:, 3 * D_MODEL:4 * D_MODEL], mod[:, 4 * D_MODEL:5 * D_MODEL], mod[:, 5 * D_MODEL:]
    h = (x * (1.0 + sc2) + sh2).astype(BF16)
    acc = jnp.zeros(x.shape, F32)
    for c in range(D_FF // FF_CHUNK):
        cols = slice(c * FF_CHUNK, (c + 1) * FF_CHUNK)
        t = jnp.dot(h, w1_ref[:, cols], preferred_element_type=F32) + b1_ref[:, cols]
        t = jnp.square(jnp.maximum(t, 0.0)).astype(BF16)
        acc = acc + jnp.dot(t, w2_ref[cols, :], preferred_element_type=F32)
    f = acc + b2_ref[...]
    o_ref[...] = _layernorm(ALPHA * x + g2 * f, g_ref[...], b_ref[...])


def _ffn(x2, mod, p, layer, *, seq_len, latent, tm):
    rows = x2.shape[0]
    mod_row = (lambda i: (i * tm) // seq_len) if latent else (lambda i: mod.shape[1] - 1)
    in_specs = [pl.BlockSpec((tm, D_MODEL), lambda i: (i, 0)),
                pl.BlockSpec((None, None, 1, 6 * D_MODEL), lambda i: (layer, mod_row(i), 0, 0))]
    args = [x2, mod]
    for name in ("w_ff1", "b_ff1", "w_ff2", "b_ff2", "ln2_g", "ln2_b"):
        in_specs.append(_layer_spec(p[name], layer))
        args.append(p[name])
    return pl.pallas_call(
        _ffn_kernel,
        grid=(rows // tm,),
        in_specs=in_specs,
        out_specs=pl.BlockSpec((tm, D_MODEL), lambda i: (i, 0)),
        out_shape=jax.ShapeDtypeStruct((rows, D_MODEL), F32),
        compiler_params=pltpu.CompilerParams(dimension_semantics=("arbitrary",), vmem_limit_bytes=VMEM_LIMIT),
        name="ffn_latent" if latent else "ffn_context",
    )(*args)


def _rope_tables(length):
    rows = length // GRID_W
    pos_row = jnp.repeat(jnp.arange(rows), GRID_W).astype(F32)
    pos_col = jnp.tile(jnp.arange(GRID_W), rows).astype(F32)
    n_f = HEAD_DIM // 4
    inv = ROPE_THETA ** (-jnp.arange(n_f, dtype=F32) / n_f)
    ang = jnp.concatenate([pos_row[:, None] * inv, pos_col[:, None] * inv], -1)
    cos = jnp.repeat(jnp.cos(ang), 2, axis=-1)
    sin = jnp.repeat(jnp.sin(ang), 2, axis=-1)
    even = (jnp.arange(HEAD_DIM) % 2 == 0)[None, :]
    sin_a = jnp.where(even, -sin, 0.0)
    sin_b = jnp.where(even, 0.0, sin)
    rep = LANES // HEAD_DIM
    return tuple(jnp.tile(t, (1, rep)).astype(F32) for t in (cos, sin_a, sin_b))


def _block_diag(w):
    eye = jnp.eye(LRU_BLOCKS, dtype=w.dtype)
    full = w[..., :, :, None, :] * eye[:, None, :, None]
    return full.reshape(w.shape[:-3] + (LRU_W, LRU_W))


def _row(a):
    return a.reshape(a.shape[0], 1, a.shape[1])


def kernel(x_prompt, x_sample, c, cache_k, cache_v, state_lru, c_ctx, w_ada, b_ada, w_in, q_norm_g, k_norm_g,
           conv_w, conv_b, lru_wa, lru_ba, lru_wx, lru_bx, lru_lam, mlp_norm_g, mlp_norm_b, mlp_ws, mlp_bs,
           w_out, ln1_g, ln1_b, w_ff1, b_ff1, w_ff2, b_ff2, ln2_g, ln2_b):
    batch, seq, _ = x_prompt.shape
    dec_batch, dec_seq, _ = x_sample.shape
    past = cache_k.shape[2]

    n_cond = dec_batch + 1
    cond_rows = -(-n_cond // SUBLANES) * SUBLANES
    cond = jnp.concatenate([c, c_ctx[None, :], jnp.zeros((cond_rows - n_cond, D_MODEL), F32)], axis=0)
    mod = _modulation(cond, w_ada, b_ada)[:, :n_cond].reshape(DEPTH, n_cond, 1, 6 * D_MODEL)

    head_id = jnp.arange(QK_W) // HEAD_DIM
    wg = jnp.stack([_block_diag(lru_wa), _block_diag(lru_wx)], axis=2)
    wg = wg.transpose(0, 3, 1, 2, 4).reshape(DEPTH, LRU_W, 4 * LRU_W)
    p = dict(
        w_in=w_in.astype(BF16),
        hsum=(head_id[:, None] == head_id[None, :]).astype(BF16),
        gqk=_row(jnp.concatenate([jnp.tile(q_norm_g, (1, N_HEADS)), jnp.tile(k_norm_g, (1, N_KV_HEADS))], axis=1)),
        conv_w=conv_w, conv_b=_row(conv_b),
        wg=wg.astype(BF16),
        bg=jnp.stack([lru_ba, lru_bx], axis=2).reshape(DEPTH, 1, 4 * LRU_W),
        lam=lru_lam,
        mlp_g=_row(mlp_norm_g), mlp_b=_row(mlp_norm_b),
        ws=mlp_ws.reshape(DEPTH, MLP_GROUPS * CHUNK, CHUNK).astype(BF16),
        bsb=jnp.repeat(mlp_bs.transpose(0, 2, 1), MLP_GW, axis=2),
        w_out=w_out.astype(BF16), ln1_g=_row(ln1_g), ln1_b=_row(ln1_b),
        w_ff1=w_ff1.astype(BF16), b_ff1=_row(b_ff1), w_ff2=w_ff2.astype(BF16), b_ff2=_row(b_ff2),
        ln2_g=_row(ln2_g), ln2_b=_row(ln2_b),
    )
    rope = _rope_tables(dec_seq)
    ctx = (cache_k.reshape(dec_batch, DEPTH, past, KV_W), cache_v.reshape(dec_batch, DEPTH, past, KV_W), state_lru)

    y_prompt = x_prompt.reshape(batch * seq, D_MODEL)
    y_sample = x_sample.reshape(dec_batch * dec_seq, D_MODEL)
    caches = None
    for l in range(DEPTH):
        x1, *caches = _mixer(y_prompt, mod, p, l, batch=batch, seq_len=seq, n_seq=2, latent=False, caches=caches)
        y_prompt = _ffn(x1, mod, p, l, seq_len=seq, latent=False, tm=512)
        (x1,) = _mixer(y_sample, mod, p, l, batch=dec_batch, seq_len=dec_seq, n_seq=1, latent=True,
                       ctx=ctx, rope=rope)
        y_sample = _ffn(x1, mod, p, l, seq_len=dec_seq, latent=True, tm=512)

    new_k, new_v, new_s = caches
    return (y_prompt.reshape(batch, seq, D_MODEL), y_sample.reshape(dec_batch, dec_seq, D_MODEL),
            new_k.reshape(batch, DEPTH, seq, N_KV_HEADS, HEAD_DIM),
            new_v.reshape(batch, DEPTH, seq, N_KV_HEADS, HEAD_DIM), new_s)
```

```python
import functools
import math

import jax
import jax.numpy as jnp
from jax import lax
from jax.experimental import pallas as pl
from jax.experimental.pallas import tpu as pltpu

D_MODEL = 1024
DEPTH = 2
GRID_W = 64
CHUNK = 128
HEAD_DIM = 64
ATTN_W = 512
N_HEADS = 8
N_KV_HEADS = 2
KV_W = 128
LRU_W = 256
LRU_BLOCKS = 4
LRU_BW = 64
CONV_W = 4
RG_C = 8.0
MLP_W = 256
MLP_GROUPS = 4
MLP_GW = 64
MIX_W = 1024
IN_W = 1792
D_FF = 4096
ROPE_THETA = 10000.0
ALPHA = (2 * DEPTH) ** 0.25
EPS = 1e-6

Q0, K0, V0, XR0, GR0, ZM0 = 0, 512, 640, 768, 1024, 1280
QK_W = ATTN_W + KV_W
LANES = 128
SUBLANES = 8
N_QGROUPS = ATTN_W // LANES
HEADS_PER_KV = N_HEADS // N_KV_HEADS
Q_BLOCK = 256
FF_CHUNK = 1024
VMEM_LIMIT = 60 * 1024 * 1024
Q_SCALE = HEAD_DIM ** -0.5 * math.log2(math.e)

F32 = jnp.float32
BF16 = jnp.bfloat16


def _layernorm(x, g, b):
    mu = jnp.mean(x, -1, keepdims=True)
    xc = x - mu
    var = jnp.mean(xc * xc, -1, keepdims=True)
    return xc * lax.rsqrt(var + EPS) * g + b


def _gelu(x):
    return jax.nn.gelu(x, approximate=True)


def _layer_spec(arr, layer):
    n = arr.ndim - 1
    return pl.BlockSpec((None,) + arr.shape[1:], lambda *_: (layer,) + (0,) * n, pipeline_mode=pl.Buffered(1))


def _const_spec(arr):
    n = arr.ndim
    return pl.BlockSpec(arr.shape, lambda *_: (0,) * n, pipeline_mode=pl.Buffered(1))


def _mod_kernel(cond_ref, w_ref, b_ref, o_ref):
    cond = cond_ref[...]
    o_ref[...] = jnp.dot(jax.nn.silu(cond), w_ref[...], preferred_element_type=F32,
                         precision=lax.Precision.HIGHEST) + b_ref[...]


def _modulation(cond, w_ada, b_ada):
    rows = cond.shape[0]
    tn = 1536
    return pl.pallas_call(
        _mod_kernel,
        grid=(DEPTH, 6 * D_MODEL // tn),
        in_specs=[pl.BlockSpec((rows, D_MODEL), lambda l, j: (0, 0)),
                  pl.BlockSpec((None, D_MODEL, tn), lambda l, j: (l, 0, j)),
                  pl.BlockSpec((None, 1, tn), lambda l, j: (l, 0, j))],
        out_specs=pl.BlockSpec((None, rows, tn), lambda l, j: (l, 0, j)),
        out_shape=jax.ShapeDtypeStruct((DEPTH, rows, 6 * D_MODEL), F32),
        compiler_params=pltpu.CompilerParams(dimension_semantics=("arbitrary", "arbitrary"),
                                             vmem_limit_bytes=VMEM_LIMIT),
        name="adaln_modulation",
    )(cond, w_ada, b_ada.reshape(DEPTH, 1, 6 * D_MODEL))


def _tile_scan(a, u, reverse):
    n_t, w = a.shape[0] // SUBLANES, a.shape[1]
    a3 = a.reshape(n_t, SUBLANES, w)
    u3 = u.reshape(n_t, SUBLANES, w)
    sub = lax.broadcasted_iota(jnp.int32, (1, SUBLANES, w), 1)
    d = 1
    while d < SUBLANES:
        keep = (sub < SUBLANES - d) if reverse else (sub >= d)
        shift = SUBLANES - d if reverse else d
        a_sh = pltpu.roll(a3, shift, 1)
        u_sh = pltpu.roll(u3, shift, 1)
        u3 = a3 * jnp.where(keep, u_sh, 0.0) + u3
        a3 = a3 * jnp.where(keep, a_sh, 1.0)
        d *= 2
    return a3.reshape(n_t * SUBLANES, w), u3.reshape(n_t * SUBLANES, w)


def _mixer_kernel(*refs, seq_len, n_seq, latent):
    it = iter(refs)
    x_ref, mod_ref, w_in_ref, hsum_ref, gqk_ref = next(it), next(it), next(it), next(it), next(it)
    if latent:
        cos_ref, sin_a_ref, sin_b_ref = next(it), next(it), next(it)
        ck_ref, cv_ref, st_ref = next(it), next(it), next(it)
    conv_w_ref, conv_b_ref, wg_ref, bg_ref, lam_ref = next(it), next(it), next(it), next(it), next(it)
    mlp_g_ref, mlp_b_ref, ws_ref, bsb_ref = next(it), next(it), next(it), next(it)
    w_out_ref, ln_g_ref, ln_b_ref = next(it), next(it), next(it)
    if not latent:
        next(it), next(it), next(it)
    x1_ref = next(it)
    if not latent:
        k_out_ref, v_out_ref, s_out_ref = next(it), next(it), next(it)
    z_s, q_s, kt_s, vp_s, o_s, mix_s = next(it), next(it), next(it), next(it), next(it), next(it)
    e_bufs, il_bufs = (next(it), next(it)), (next(it), next(it))
    af_s, uf_s, ab_s, ub_s = next(it), next(it), next(it), next(it)

    L = seq_len
    past = ck_ref.shape[0] if latent else 0
    T = past + L
    n_tiles = L // SUBLANES
    kv_w = HEADS_PER_KV * HEAD_DIM

    mod = mod_ref[...]
    sh1, sc1, g1 = mod[:, 0:D_MODEL], mod[:, D_MODEL:2 * D_MODEL], mod[:, 2 * D_MODEL:3 * D_MODEL]
    h = (x_ref[...] * (1.0 + sc1) + sh1).astype(BF16)
    zqkv = jnp.dot(h, w_in_ref[:, :XR0], preferred_element_type=F32)
    z_s[...] = jnp.dot(h, w_in_ref[:, XR0:], preferred_element_type=F32)

    qk = zqkv[:, Q0:Q0 + QK_W]
    ssq = jnp.dot((qk * qk).astype(BF16), hsum_ref[...], preferred_element_type=F32)
    qkn = qk * lax.rsqrt(ssq * (1.0 / HEAD_DIM) + EPS) * gqk_ref[...]

    lam = lam_ref[...]
    neg = -lam
    decay = -RG_C * (jnp.maximum(neg, 0.0) + jnp.log1p(jnp.exp(-jnp.abs(neg))))
    quarter = lax.broadcasted_iota(jnp.int32, (1, kv_w), 1) // HEAD_DIM

    for s in range(n_seq):
        r0 = s * L
        rows = slice(r0, r0 + L)
        qkn_s = qkn[rows]
        k_cur = qkn_s[:, ATTN_W:]
        v_cur = zqkv[rows, V0:V0 + KV_W]
        if latent:
            cos, sin_a, sin_b = cos_ref[...], sin_a_ref[...], sin_b_ref[...]

            def rope(t):
                return (t * cos + pltpu.roll(t, LANES - 1, 1) * sin_a + pltpu.roll(t, 1, 1) * sin_b)

            for g in range(N_QGROUPS):
                qg = rope(qkn_s[:, g * LANES:(g + 1) * LANES]) * Q_SCALE
                q_s[g, rows, :] = qg.astype(BF16)
            k_all = jnp.concatenate([ck_ref[...], rope(k_cur)], axis=0)
            v_all = jnp.concatenate([cv_ref[...], v_cur], axis=0)
        else:
            k_out_ref[s] = k_cur
            v_out_ref[s] = v_cur
            for g in range(N_QGROUPS):
                q_s[g, rows, :] = (qkn_s[:, g * LANES:(g + 1) * LANES] * Q_SCALE).astype(BF16)
            k_all, v_all = k_cur, v_cur

        kt = k_all.T
        zer = jnp.zeros((HEAD_DIM, T), F32)
        for j in range(N_KV_HEADS):
            kj = kt[j * HEAD_DIM:(j + 1) * HEAD_DIM]
            kt_s[s, 2 * j] = jnp.concatenate([kj, zer], axis=0).astype(BF16)
            kt_s[s, 2 * j + 1] = jnp.concatenate([zer, kj], axis=0).astype(BF16)
        v_rot = pltpu.roll(v_all, HEAD_DIM, 1)
        v_two = jnp.concatenate([v_all, v_all], axis=1)
        v_rot_two = jnp.concatenate([v_rot, v_rot], axis=1)
        for j in range(N_KV_HEADS):
            for hq in range(HEADS_PER_KV):
                src = v_two if hq % 2 == j else v_rot_two
                vp_s[s, j, hq] = jnp.where(quarter == hq, src, 0.0).astype(BF16)

    qb = min(Q_BLOCK, L)
    n_qb = L // qb
    n_units = n_seq * N_KV_HEADS * n_qb

    def unit_index(n):
        s = n // (N_KV_HEADS * n_qb)
        j = (n // n_qb) % N_KV_HEADS
        qrows = pl.ds(pl.multiple_of(s * L + (n % n_qb) * qb, qb), qb)
        return s, j, qrows

    def score_stage(n, e_ref, il_ref):
        s, j, qrows = unit_index(n)
        inv = None
        for hq in range(HEADS_PER_KV):
            sc = jnp.dot(q_s[2 * j + hq // 2, qrows, :], kt_s[s, 2 * j + hq % 2],
                         preferred_element_type=F32)
            e = jnp.exp2(sc - jnp.max(sc, -1, keepdims=True))
            e_ref[hq] = e.astype(BF16)
            il = 1.0 / jnp.sum(e, -1, keepdims=True)
            inv = il if inv is None else jnp.where(quarter >= hq, il, inv)
        il_ref[...] = inv

    def value_stage(n, e_ref, il_ref):
        s, j, qrows = unit_index(n)
        acc = None
        for hq in range(HEADS_PER_KV):
            o = jnp.dot(e_ref[hq], vp_s[s, j, hq], preferred_element_type=F32)
            acc = o if acc is None else acc + o
        o_s[j, qrows, :] = (acc * il_ref[...]).astype(BF16)

    def unit_pair(k, carry):
        score_stage(2 * k + 1, e_bufs[1], il_bufs[1])
        value_stage(2 * k, e_bufs[0], il_bufs[0])
        score_stage(2 * k + 2, e_bufs[0], il_bufs[0])
        value_stage(2 * k + 1, e_bufs[1], il_bufs[1])
        return carry

    score_stage(0, e_bufs[0], il_bufs[0])
    lax.fori_loop(0, n_units // 2 - 1, unit_pair, 0)
    score_stage(n_units - 1, e_bufs[1], il_bufs[1])
    value_stage(n_units - 2, e_bufs[0], il_bufs[0])
    value_stage(n_units - 1, e_bufs[1], il_bufs[1])

    for s in range(n_seq):
        r0 = s * L
        rows = slice(r0, r0 + L)
        xr = z_s[rows, 0:LRU_W]
        row = lax.broadcasted_iota(jnp.int32, (L, LRU_W), 0)
        cw = conv_w_ref[...]
        xc = (jnp.where(row >= 1, pltpu.roll(xr, 1, 0), 0.0) * cw[0:1]
              + xr * cw[1:2]
              + jnp.where(row < L - 1, pltpu.roll(xr, L - 1, 0), 0.0) * cw[2:3]
              + jnp.where(row < L - 2, pltpu.roll(xr, L - 2, 0), 0.0) * cw[3:4]
              + conv_b_ref[...])
        xcb = xc.astype(BF16)
        for d, (a_s, u_s) in enumerate(((af_s, uf_s), (ab_s, ub_s))):
            gates = jnp.dot(xcb, wg_ref[:, 2 * d * LRU_W:2 * (d + 1) * LRU_W],
                            preferred_element_type=F32) + bg_ref[:, 2 * d * LRU_W:2 * (d + 1) * LRU_W]
            r = jax.nn.sigmoid(gates[:, :LRU_W])
            i_g = jax.nn.sigmoid(gates[:, LRU_W:])
            log_a = decay[d:d + 1] * r
            a = jnp.exp(log_a)
            th = jnp.tanh(log_a)
            u = jnp.sqrt(-2.0 * th / (1.0 - th)) * i_g * xc
            a_t, u_t = _tile_scan(a, u, reverse=(d == 1))
            a_s[rows, :] = a_t
            u_s[rows, :] = u_t

        def tile_step(t, carry):
            cf, cb = carry
            rf = pl.ds(pl.multiple_of(r0 + t * SUBLANES, SUBLANES), SUBLANES)
            rb = pl.ds(pl.multiple_of(r0 + (n_tiles - 1 - t) * SUBLANES, SUBLANES), SUBLANES)
            hf = af_s[rf, :] * cf + uf_s[rf, :]
            hb = ab_s[rb, :] * cb + ub_s[rb, :]
            uf_s[rf, :] = hf
            ub_s[rb, :] = hb
            return (jnp.broadcast_to(hf[SUBLANES - 1:SUBLANES], (SUBLANES, LRU_W)),
                    jnp.broadcast_to(hb[0:1], (SUBLANES, LRU_W)))

        if latent:
            init = (jnp.broadcast_to(st_ref[0:1, :], (SUBLANES, LRU_W)),
                    jnp.broadcast_to(st_ref[1:2, :], (SUBLANES, LRU_W)))
        else:
            init = (jnp.zeros((SUBLANES, LRU_W), F32), jnp.zeros((SUBLANES, LRU_W), F32))
        fin_f, fin_b = lax.fori_loop(0, n_tiles, tile_step, init, unroll=4)
        if not latent:
            s_out_ref[s, 0:1, :] = fin_f[0:1]
            s_out_ref[s, 1:2, :] = fin_b[0:1]
        y = (uf_s[rows, :] + ub_s[rows, :]) * _gelu(z_s[rows, GR0 - XR0:GR0 - XR0 + LRU_W])
        mix_s[rows, 0:LRU_W] = y.astype(BF16)

        glane = lax.broadcasted_iota(jnp.int32, (CHUNK, MLP_W), 1) // MLP_GW
        for c in range(L // CHUNK):
            crow = slice(r0 + c * CHUNK, r0 + (c + 1) * CHUNK)
            zg = _gelu(z_s[crow, ZM0 - XR0:ZM0 - XR0 + 2 * MLP_W])
            u_m = zg[:, :MLP_W]
            vn = _layernorm(zg[:, MLP_W:], mlp_g_ref[...], mlp_b_ref[...]).astype(BF16)
            full = jnp.dot(ws_ref[...], vn, preferred_element_type=F32)
            sm = bsb_ref[...]
            for gi in range(MLP_GROUPS):
                sm = sm + jnp.where(glane == gi, full[gi * CHUNK:(gi + 1) * CHUNK], 0.0)
            mix_s[crow, LRU_W:] = (u_m * sm).astype(BF16)

    out = jnp.dot(mix_s[...], w_out_ref[ATTN_W:, :], preferred_element_type=F32)
    for j in range(N_KV_HEADS):
        out = out + jnp.dot(o_s[j], w_out_ref[j * kv_w:(j + 1) * kv_w, :], preferred_element_type=F32)
    x1_ref[...] = _layernorm(ALPHA * x_ref[...] + g1 * out, ln_g_ref[...], ln_b_ref[...])


def _mixer(x2, mod, p, layer, *, batch, seq_len, n_seq, latent, ctx=None, rope=None, caches=None):
    B, L = batch, seq_len
    M = n_seq * L
    n_steps = B // n_seq
    T = L + (ctx[0].shape[2] if latent else 0)

    mod_row = (lambda i: i) if latent else (lambda i: mod.shape[1] - 1)
    in_specs = [pl.BlockSpec((M, D_MODEL), lambda i: (i, 0)),
                pl.BlockSpec((None, None, 1, 6 * D_MODEL), lambda i: (layer, mod_row(i), 0, 0))]
    args = [x2, mod]
    for name in ("w_in", "hsum", "gqk"):
        a = p[name]
        in_specs.append(_const_spec(a) if name == "hsum" else _layer_spec(a, layer))
        args.append(a)
    if latent:
        for a in rope:
            in_specs.append(_const_spec(a))
            args.append(a)
        ck, cv, st = ctx
        past = ck.shape[2]
        in_specs += [pl.BlockSpec((None, None, past, KV_W), lambda i: (i, layer, 0, 0)),
                     pl.BlockSpec((None, None, past, KV_W), lambda i: (i, layer, 0, 0)),
                     pl.BlockSpec((None, None, 2, LRU_W), lambda i: (i, layer, 0, 0))]
        args += [ck, cv, st]
    for name in ("conv_w", "conv_b", "wg", "bg", "lam", "mlp_g", "mlp_b", "ws", "bsb", "w_out", "ln1_g", "ln1_b"):
        in_specs.append(_layer_spec(p[name], layer))
        args.append(p[name])

    out_shape = [jax.ShapeDtypeStruct((B * L, D_MODEL), F32)]
    out_specs = [pl.BlockSpec((M, D_MODEL), lambda i: (i, 0))]
    aliases = {}
    if not latent:
        if caches is None:
            caches = (jnp.zeros((B, DEPTH, L, KV_W), F32), jnp.zeros((B, DEPTH, L, KV_W), F32),
                      jnp.zeros((B, DEPTH, 2, LRU_W), F32))
        for c in caches:
            aliases[len(args)] = len(out_shape)
            in_specs.append(pl.BlockSpec(memory_space=pl.ANY))
            args.append(c)
            out_shape.append(jax.ShapeDtypeStruct(c.shape, F32))
        out_specs += [pl.BlockSpec((n_seq, None, L, KV_W), lambda i: (i, layer, 0, 0)),
                      pl.BlockSpec((n_seq, None, L, KV_W), lambda i: (i, layer, 0, 0)),
                      pl.BlockSpec((n_seq, None, 2, LRU_W), lambda i: (i, layer, 0, 0))]

    qb = min(Q_BLOCK, L)
    kv_w = HEADS_PER_KV * HEAD_DIM
    scratch = [pltpu.VMEM((M, IN_W - XR0), F32),
               pltpu.VMEM((N_QGROUPS, M, LANES), BF16),
               pltpu.VMEM((n_seq, 2 * N_KV_HEADS, LANES, T), BF16),
               pltpu.VMEM((n_seq, N_KV_HEADS, HEADS_PER_KV, T, kv_w), BF16),
               pltpu.VMEM((N_KV_HEADS, M, kv_w), BF16),
               pltpu.VMEM((M, LRU_W + MLP_W), BF16)]
    scratch += [pltpu.VMEM((HEADS_PER_KV, qb, T), BF16)] * 2
    scratch += [pltpu.VMEM((qb, kv_w), F32)] * 2
    scratch += [pltpu.VMEM((M, LRU_W), F32)] * 4

    return pl.pallas_call(
        functools.partial(_mixer_kernel, seq_len=L, n_seq=n_seq, latent=latent),
        grid=(n_steps,),
        in_specs=in_specs,
        out_specs=out_specs,
        out_shape=out_shape,
        scratch_shapes=scratch,
        input_output_aliases=aliases,
        compiler_params=pltpu.CompilerParams(dimension_semantics=("arbitrary",), vmem_limit_bytes=VMEM_LIMIT),
        name="mixer_latent" if latent else "mixer_context",
    )(*args)


def _ffn_kernel(x_ref, mod_ref, w1_ref, b1_ref, w2_ref, b2_ref, g_ref, b_ref, o_ref):
    x = x_ref[...]
    mod = mod_ref[...]
    sh2, sc2, g2 = mod[:, 3 * D_MODEL:4 * D_MODEL], mod[:, 4 * D_MODEL:5 * D_MODEL], mod[:, 5 * D_MODEL:]
    h = (x * (1.0 + sc2) + sh2).astype(BF16)
    acc = jnp.zeros(x.shape, F32)
    for c in range(D_FF // FF_CHUNK):
        cols = slice(c * FF_CHUNK, (c + 1) * FF_CHUNK)
        t = jnp.dot(h, w1_ref[:, cols], preferred_element_type=F32) + b1_ref[:, cols]
        t = jnp.square(jnp.maximum(t, 0.0)).astype(BF16)
        acc = acc + jnp.dot(t, w2_ref[cols, :], preferred_element_type=F32)
    f = acc + b2_ref[...]
    o_ref[...] = _layernorm(ALPHA * x + g2 * f, g_ref[...], b_ref[...])


def _ffn(x2, mod, p, layer, *, seq_len, latent, tm):
    rows = x2.shape[0]
    mod_row = (lambda i: (i * tm) // seq_len) if latent else (lambda i: mod.shape[1] - 1)
    in_specs = [pl.BlockSpec((tm, D_MODEL), lambda i: (i, 0)),
                pl.BlockSpec((None, None, 1, 6 * D_MODEL), lambda i: (layer, mod_row(i), 0, 0))]
    args = [x2, mod]
    for name in ("w_ff1", "b_ff1", "w_ff2", "b_ff2", "ln2_g", "ln2_b"):
        in_specs.append(_layer_spec(p[name], layer))
        args.append(p[name])
    return pl.pallas_call(
        _ffn_kernel,
        grid=(rows // tm,),
        in_specs=in_specs,
        out_specs=pl.BlockSpec((tm, D_MODEL), lambda i: (i, 0)),
        out_shape=jax.ShapeDtypeStruct((rows, D_MODEL), F32),
        compiler_params=pltpu.CompilerParams(dimension_semantics=("arbitrary",), vmem_limit_bytes=VMEM_LIMIT),
        name="ffn_latent" if latent else "ffn_context",
    )(*args)


def _rope_tables(length):
    rows = length // GRID_W
    pos_row = jnp.repeat(jnp.arange(rows), GRID_W).astype(F32)
    pos_col = jnp.tile(jnp.arange(GRID_W), rows).astype(F32)
    n_f = HEAD_DIM // 4
    inv = ROPE_THETA ** (-jnp.arange(n_f, dtype=F32) / n_f)
    ang = jnp.concatenate([pos_row[:, None] * inv, pos_col[:, None] * inv], -1)
    cos = jnp.repeat(jnp.cos(ang), 2, axis=-1)
    sin = jnp.repeat(jnp.sin(ang), 2, axis=-1)
    even = (jnp.arange(HEAD_DIM) % 2 == 0)[None, :]
    sin_a = jnp.where(even, -sin, 0.0)
    sin_b = jnp.where(even, 0.0, sin)
    rep = LANES // HEAD_DIM
    return tuple(jnp.tile(t, (1, rep)).astype(F32) for t in (cos, sin_a, sin_b))


def _block_diag(w):
    eye = jnp.eye(LRU_BLOCKS, dtype=w.dtype)
    full = w[..., :, :, None, :] * eye[:, None, :, None]
    return full.reshape(w.shape[:-3] + (LRU_W, LRU_W))


def _row(a):
    return a.reshape(a.shape[0], 1, a.shape[1])


def kernel(x_prompt, x_sample, c, cache_k, cache_v, state_lru, c_ctx, w_ada, b_ada, w_in, q_norm_g, k_norm_g,
           conv_w, conv_b, lru_wa, lru_ba, lru_wx, lru_bx, lru_lam, mlp_norm_g, mlp_norm_b, mlp_ws, mlp_bs,
           w_out, ln1_g, ln1_b, w_ff1, b_ff1, w_ff2, b_ff2, ln2_g, ln2_b):
    batch, seq, _ = x_prompt.shape
    dec_batch, dec_seq, _ = x_sample.shape
    past = cache_k.shape[2]

    n_cond = dec_batch + 1
    cond_rows = -(-n_cond // SUBLANES) * SUBLANES
    cond = jnp.concatenate([c, c_ctx[None, :], jnp.zeros((cond_rows - n_cond, D_MODEL), F32)], axis=0)
    mod = _modulation(cond, w_ada, b_ada)[:, :n_cond].reshape(DEPTH, n_cond, 1, 6 * D_MODEL)

    head_id = jnp.arange(QK_W) // HEAD_DIM
    wg = jnp.stack([_block_diag(lru_wa), _block_diag(lru_wx)], axis=2)
    wg = wg.transpose(0, 3, 1, 2, 4).reshape(DEPTH, LRU_W, 4 * LRU_W)
    p = dict(
        w_in=w_in.astype(BF16),
        hsum=(head_id[:, None] == head_id[None, :]).astype(BF16),
        gqk=_row(jnp.concatenate([jnp.tile(q_norm_g, (1, N_HEADS)), jnp.tile(k_norm_g, (1, N_KV_HEADS))], axis=1)),
        conv_w=conv_w, conv_b=_row(conv_b),
        wg=wg.astype(BF16),
        bg=jnp.stack([lru_ba, lru_bx], axis=2).reshape(DEPTH, 1, 4 * LRU_W),
        lam=lru_lam,
        mlp_g=_row(mlp_norm_g), mlp_b=_row(mlp_norm_b),
        ws=mlp_ws.reshape(DEPTH, MLP_GROUPS * CHUNK, CHUNK).astype(BF16),
        bsb=jnp.repeat(mlp_bs.transpose(0, 2, 1), MLP_GW, axis=2),
        w_out=w_out.astype(BF16), ln1_g=_row(ln1_g), ln1_b=_row(ln1_b),
        w_ff1=w_ff1.astype(BF16), b_ff1=_row(b_ff1), w_ff2=w_ff2.astype(BF16), b_ff2=_row(b_ff2),
        ln2_g=_row(ln2_g), ln2_b=_row(ln2_b),
    )
    rope = _rope_tables(dec_seq)
    ctx = (cache_k.reshape(dec_batch, DEPTH, past, KV_W), cache_v.reshape(dec_batch, DEPTH, past, KV_W), state_lru)

    y_prompt = x_prompt.reshape(batch * seq, D_MODEL)
    y_sample = x_sample.reshape(dec_batch * dec_seq, D_MODEL)
    caches = None
    for l in range(DEPTH):
        x1, *caches = _mixer(y_prompt, mod, p, l, batch=batch, seq_len=seq, n_seq=2, latent=False, caches=caches)
        y_prompt = _ffn(x1, mod, p, l, seq_len=seq, latent=False, tm=512)
        (x1,) = _mixer(y_sample, mod, p, l, batch=dec_batch, seq_len=dec_seq, n_seq=1, latent=True,
                       ctx=ctx, rope=rope)
        y_sample = _ffn(x1, mod, p, l, seq_len=dec_seq, latent=True, tm=512)

    new_k, new_v, new_s = caches
    return (y_prompt.reshape(batch, seq, D_MODEL), y_sample.reshape(dec_batch, dec_seq, D_MODEL),
            new_k.reshape(batch, DEPTH, seq, N_KV_HEADS, HEAD_DIM),
            new_v.reshape(batch, DEPTH, seq, N_KV_HEADS, HEAD_DIM), new_s)
```

```python
import functools
import math

import jax
import jax.numpy as jnp
from jax import lax
from jax.experimental import pallas as pl
from jax.experimental.pallas import tpu as pltpu

D_MODEL = 1024
DEPTH = 2
GRID_W = 64
CHUNK = 128
HEAD_DIM = 64
ATTN_W = 512
N_HEADS = 8
N_KV_HEADS = 2
KV_W = 128
LRU_W = 256
LRU_BLOCKS = 4
LRU_BW = 64
CONV_W = 4
RG_C = 8.0
MLP_W = 256
MLP_GROUPS = 4
MLP_GW = 64
MIX_W = 1024
IN_W = 1792
D_FF = 4096
ROPE_THETA = 10000.0
ALPHA = (2 * DEPTH) ** 0.25
EPS = 1e-6

Q0, K0, V0, XR0, GR0, ZM0 = 0, 512, 640, 768, 1024, 1280
QK_W = ATTN_W + KV_W
LANES = 128
SUBLANES = 8
N_QGROUPS = ATTN_W // LANES
HEADS_PER_KV = N_HEADS // N_KV_HEADS
Q_BLOCK = 256
ROW_CHUNK = 256
FF_CHUNK = 1024
FF_TILE = 1024
FF_ROWS = 512
VMEM_LIMIT = 60 * 1024 * 1024
Q_SCALE = HEAD_DIM ** -0.5 * math.log2(math.e)

F32 = jnp.float32
BF16 = jnp.bfloat16


def _layernorm(x, g, b):
    mu = jnp.mean(x, -1, keepdims=True)
    xc = x - mu
    var = jnp.mean(xc * xc, -1, keepdims=True)
    return xc * lax.rsqrt(var + EPS) * g + b


_GELU_C = math.sqrt(2.0 / math.pi)


def _gelu(x):
    k = -2.0 * _GELU_C * math.log2(math.e)
    return x / (1.0 + jnp.exp2(x * (k + (k * 0.044715) * (x * x))))


def _layer_spec(arr, layer):
    n = arr.ndim - 1
    return pl.BlockSpec((None,) + arr.shape[1:], lambda *_: (layer,) + (0,) * n, pipeline_mode=pl.Buffered(1))


def _const_spec(arr):
    n = arr.ndim
    return pl.BlockSpec(arr.shape, lambda *_: (0,) * n, pipeline_mode=pl.Buffered(1))


def _mod_kernel(cond_ref, w_ref, b_ref, o_ref):
    cond = cond_ref[...]
    o_ref[...] = jnp.dot(jax.nn.silu(cond), w_ref[...], preferred_element_type=F32,
                         precision=lax.Precision.HIGHEST) + b_ref[...]


def _modulation(cond, w_ada, b_ada):
    rows = cond.shape[0]
    tn = 1536
    return pl.pallas_call(
        _mod_kernel,
        grid=(DEPTH, 6 * D_MODEL // tn),
        in_specs=[pl.BlockSpec((rows, D_MODEL), lambda l, j: (0, 0)),
                  pl.BlockSpec((None, D_MODEL, tn), lambda l, j: (l, 0, j)),
                  pl.BlockSpec((None, 1, tn), lambda l, j: (l, 0, j))],
        out_specs=pl.BlockSpec((None, rows, tn), lambda l, j: (l, 0, j)),
        out_shape=jax.ShapeDtypeStruct((DEPTH, rows, 6 * D_MODEL), F32),
        compiler_params=pltpu.CompilerParams(dimension_semantics=("arbitrary", "arbitrary"),
                                             vmem_limit_bytes=VMEM_LIMIT),
        name="adaln_modulation",
    )(cond, w_ada, b_ada.reshape(DEPTH, 1, 6 * D_MODEL))


def _tile_scan(a, u, reverse):
    n_t, w = a.shape[0] // SUBLANES, a.shape[1]
    a3 = a.reshape(n_t, SUBLANES, w)
    u3 = u.reshape(n_t, SUBLANES, w)
    sub = lax.broadcasted_iota(jnp.int32, (1, SUBLANES, w), 1)
    d = 1
    while d < SUBLANES:
        keep = (sub < SUBLANES - d) if reverse else (sub >= d)
        shift = SUBLANES - d if reverse else d
        a_sh = pltpu.roll(a3, shift, 1)
        u_sh = pltpu.roll(u3, shift, 1)
        u3 = a3 * jnp.where(keep, u_sh, 0.0) + u3
        a3 = a3 * jnp.where(keep, a_sh, 1.0)
        d *= 2
    return a3.reshape(n_t * SUBLANES, w), u3.reshape(n_t * SUBLANES, w)


def _mixer_kernel(*refs, seq_len, n_seq, latent):
    it = iter(refs)
    x_ref, mod_ref, w_in_ref, hsum_ref, gqk_ref = next(it), next(it), next(it), next(it), next(it)
    if latent:
        cos_ref, sin_a_ref, sin_b_ref = next(it), next(it), next(it)
        ck_ref, cv_ref, st_ref = next(it), next(it), next(it)
    conv_w_ref, conv_b_ref, wg_ref, bg_ref, lam_ref = next(it), next(it), next(it), next(it), next(it)
    mlp_g_ref, mlp_b_ref, ws_ref, bsb_ref = next(it), next(it), next(it), next(it)
    w_out_ref, ln_g_ref, ln_b_ref = next(it), next(it), next(it)
    if not latent:
        next(it), next(it), next(it)
    x1_ref = next(it)
    if not latent:
        k_out_ref, v_out_ref, s_out_ref = next(it), next(it), next(it)
    z_s, q_s, kt_s, vp_s, o_s, mix_s, h_s = (next(it) for _ in range(7))
    e_bufs, il_bufs = (next(it), next(it)), (next(it), next(it))
    af_s, uf_s, ab_s, ub_s = next(it), next(it), next(it), next(it)

    L = seq_len
    past = ck_ref.shape[0] if latent else 0
    n_tiles = L // SUBLANES
    kv_w = HEADS_PER_KV * HEAD_DIM
    rc = min(ROW_CHUNK, L)
    chunks_per_seq = L // rc
    n_chunks = n_seq * chunks_per_seq

    mod = mod_ref[...]
    sh1, sc1, g1 = mod[:, 0:D_MODEL], mod[:, D_MODEL:2 * D_MODEL], mod[:, 2 * D_MODEL:3 * D_MODEL]
    lam = lam_ref[...]
    neg = -lam
    decay = -RG_C * (jnp.maximum(neg, 0.0) + jnp.log1p(jnp.exp(-jnp.abs(neg))))
    quarter = lax.broadcasted_iota(jnp.int32, (1, kv_w), 1) // HEAD_DIM

    def put_keys_values(s, k_rows, v_rows, t0):
        n = k_rows.shape[0]
        kt = k_rows.T.astype(BF16)
        zer = jnp.zeros((HEAD_DIM, n), BF16)
        v_rot = pltpu.roll(v_rows, HEAD_DIM, 1)
        v_two = jnp.concatenate([v_rows, v_rows], axis=1)
        v_rot_two = jnp.concatenate([v_rot, v_rot], axis=1)
        for j in range(N_KV_HEADS):
            kj = kt[j * HEAD_DIM:(j + 1) * HEAD_DIM]
            kt_s[s, 2 * j, :, t0:t0 + n] = jnp.concatenate([kj, zer], axis=0)
            kt_s[s, 2 * j + 1, :, t0:t0 + n] = jnp.concatenate([zer, kj], axis=0)
            for hq in range(HEADS_PER_KV):
                src = v_two if hq % 2 == j else v_rot_two
                vp_s[s, j, hq, t0:t0 + n, :] = jnp.where(quarter == hq, src, 0.0).astype(BF16)

    def project_chunk(c):
        s, pos = c // chunks_per_seq, (c % chunks_per_seq) * rc
        rows = slice(c * rc, (c + 1) * rc)
        h_s[rows, :] = (x_ref[rows, :] * (1.0 + sc1) + sh1).astype(BF16)
        zqkv = jnp.dot(h_s[rows, :], w_in_ref[:, :XR0], preferred_element_type=F32)
        z_s[rows, :] = jnp.dot(h_s[rows, :], w_in_ref[:, XR0:], preferred_element_type=F32)

        qk = zqkv[:, Q0:Q0 + QK_W]
        ssq = jnp.dot((qk * qk).astype(BF16), hsum_ref[...], preferred_element_type=F32)
        qkn = qk * lax.rsqrt(ssq * (1.0 / HEAD_DIM) + EPS) * gqk_ref[...]
        k_cur = qkn[:, ATTN_W:]
        v_cur = zqkv[:, V0:V0 + KV_W]
        if latent:
            cos, sin_a, sin_b = cos_ref[pos:pos + rc, :], sin_a_ref[pos:pos + rc, :], sin_b_ref[pos:pos + rc, :]

            def rope(t):
                return t * cos + pltpu.roll(t, LANES - 1, 1) * sin_a + pltpu.roll(t, 1, 1) * sin_b

            for g in range(N_QGROUPS):
                q_s[g, rows, :] = rope(qkn[:, g * LANES:(g + 1) * LANES]).astype(BF16)
            put_keys_values(s, rope(k_cur), v_cur, past + pos)
        else:
            k_out_ref[s, pos:pos + rc, :] = k_cur
            v_out_ref[s, pos:pos + rc, :] = v_cur
            for g in range(N_QGROUPS):
                q_s[g, rows, :] = qkn[:, g * LANES:(g + 1) * LANES].astype(BF16)
            put_keys_values(s, k_cur, v_cur, pos)

        gr = slice(GR0 - XR0, GR0 - XR0 + LRU_W)
        z_s[rows, gr] = _gelu(z_s[rows, gr])

        glane = lax.broadcasted_iota(jnp.int32, (CHUNK, MLP_W), 1) // MLP_GW
        for cc in range(rc // CHUNK):
            crow = slice(c * rc + cc * CHUNK, c * rc + (cc + 1) * CHUNK)
            zg = _gelu(z_s[crow, ZM0 - XR0:ZM0 - XR0 + 2 * MLP_W])
            u_m = zg[:, :MLP_W]
            vn = _layernorm(zg[:, MLP_W:], mlp_g_ref[...], mlp_b_ref[...]).astype(BF16)
            full = jnp.dot(ws_ref[...], vn, preferred_element_type=F32)
            sm = bsb_ref[...]
            for gi in range(MLP_GROUPS):
                sm = sm + jnp.where(glane == gi, full[gi * CHUNK:(gi + 1) * CHUNK], 0.0)
            mix_s[crow, :] = (u_m * sm).astype(BF16)

    def lru_chunk(c):
        pos = (c % chunks_per_seq) * rc
        r0 = c * rc
        rows = slice(r0, r0 + rc)
        xr = z_s[rows, 0:LRU_W]
        row = lax.broadcasted_iota(jnp.int32, (rc, LRU_W), 0)
        zero_row = jnp.zeros((1, LRU_W), F32)
        before = z_s[r0 - 1:r0, 0:LRU_W] if pos > 0 else zero_row
        after1 = z_s[r0 + rc:r0 + rc + 1, 0:LRU_W] if pos + rc < L else zero_row
        after2 = z_s[r0 + rc + 1:r0 + rc + 2, 0:LRU_W] if pos + rc < L else zero_row
        x_m1 = jnp.where(row >= 1, pltpu.roll(xr, 1, 0), before)
        x_p1 = jnp.where(row < rc - 1, pltpu.roll(xr, rc - 1, 0), after1)
        x_p2 = jnp.where(row < rc - 2, pltpu.roll(xr, rc - 2, 0), jnp.where(row == rc - 2, after1, after2))
        cw = conv_w_ref[...]
        xc = x_m1 * cw[0:1] + xr * cw[1:2] + x_p1 * cw[2:3] + x_p2 * cw[3:4] + conv_b_ref[...]
        xcb = xc.astype(BF16)
        xc2 = xc * math.sqrt(2.0)
        for d, (a_s, u_s) in enumerate(((af_s, uf_s), (ab_s, ub_s))):
            gates = jnp.dot(xcb, wg_ref[:, 2 * d * LRU_W:2 * (d + 1) * LRU_W],
                            preferred_element_type=F32) + bg_ref[:, 2 * d * LRU_W:2 * (d + 1) * LRU_W]
            r = jax.nn.sigmoid(gates[:, :LRU_W])
            i_g = jax.nn.sigmoid(gates[:, LRU_W:])
            log_a = decay[d:d + 1] * r
            a = jnp.exp(log_a)
            th = jnp.tanh(log_a)
            w = th / (th - 1.0)
            u = jnp.where(w > 0.0, w * lax.rsqrt(w), 0.0) * i_g * xc2
            a_t, u_t = _tile_scan(a, u, reverse=(d == 1))
            a_s[rows, :] = a_t
            u_s[rows, :] = u_t

    if latent:
        put_keys_values(0, ck_ref[...], cv_ref[...], 0)
    for c in range(n_chunks):
        project_chunk(c)
        if c >= 1:
            lru_chunk(c - 1)
    lru_chunk(n_chunks - 1)

    qb = min(Q_BLOCK, L)
    n_qb = L // qb
    n_units = n_seq * N_KV_HEADS * n_qb

    def unit_index(n):
        s = n // (N_KV_HEADS * n_qb)
        j = (n // n_qb) % N_KV_HEADS
        qrows = pl.ds(pl.multiple_of(s * L + (n % n_qb) * qb, qb), qb)
        return s, j, qrows

    def score_stage(n, e_ref, il_ref):
        s, j, qrows = unit_index(n)
        inv = None
        for hq in range(HEADS_PER_KV):
            sc = jnp.dot(q_s[2 * j + hq // 2, qrows, :], kt_s[s, 2 * j + hq % 2],
                         preferred_element_type=F32)
            e = jnp.exp2(sc - jnp.max(sc, -1, keepdims=True))
            e_ref[hq] = e.astype(BF16)
            il = 1.0 / jnp.sum(e, -1, keepdims=True)
            inv = il if inv is None else jnp.where(quarter >= hq, il, inv)
        il_ref[...] = inv

    def value_stage(n, e_ref, il_ref):
        s, j, qrows = unit_index(n)
        acc = None
        for hq in range(HEADS_PER_KV):
            o = jnp.dot(e_ref[hq], vp_s[s, j, hq], preferred_element_type=F32)
            acc = o if acc is None else acc + o
        o_s[j, qrows, :] = (acc * il_ref[...]).astype(BF16)

    def unit_pair(k, carry):
        score_stage(2 * k + 1, e_bufs[1], il_bufs[1])
        value_stage(2 * k, e_bufs[0], il_bufs[0])
        score_stage(2 * k + 2, e_bufs[0], il_bufs[0])
        value_stage(2 * k + 1, e_bufs[1], il_bufs[1])
        return carry

    score_stage(0, e_bufs[0], il_bufs[0])
    lax.fori_loop(0, n_units // 2 - 1, unit_pair, 0)
    score_stage(n_units - 1, e_bufs[1], il_bufs[1])
    value_stage(n_units - 2, e_bufs[0], il_bufs[0])
    value_stage(n_units - 1, e_bufs[1], il_bufs[1])

    for s in range(n_seq):
        r0 = s * L

        def tile_step(t, carry, r0=r0):
            cf, cb = carry
            rf = pl.ds(pl.multiple_of(r0 + t * SUBLANES, SUBLANES), SUBLANES)
            rb = pl.ds(pl.multiple_of(r0 + (n_tiles - 1 - t) * SUBLANES, SUBLANES), SUBLANES)
            hf = af_s[rf, :] * cf + uf_s[rf, :]
            hb = ab_s[rb, :] * cb + ub_s[rb, :]
            uf_s[rf, :] = hf
            ub_s[rb, :] = hb
            return (jnp.broadcast_to(hf[SUBLANES - 1:SUBLANES], (SUBLANES, LRU_W)),
                    jnp.broadcast_to(hb[0:1], (SUBLANES, LRU_W)))

        if latent:
            init = (jnp.broadcast_to(st_ref[0:1, :], (SUBLANES, LRU_W)),
                    jnp.broadcast_to(st_ref[1:2, :], (SUBLANES, LRU_W)))
        else:
            init = (jnp.zeros((SUBLANES, LRU_W), F32), jnp.zeros((SUBLANES, LRU_W), F32))
        fin_f, fin_b = lax.fori_loop(0, n_tiles, tile_step, init, unroll=4)
        if not latent:
            s_out_ref[s, 0:1, :] = fin_f[0:1]
            s_out_ref[s, 1:2, :] = fin_b[0:1]

    for c in range(n_chunks):
        rows = slice(c * rc, (c + 1) * rc)
        y = ((uf_s[rows, :] + ub_s[rows, :]) * z_s[rows, GR0 - XR0:GR0 - XR0 + LRU_W]).astype(BF16)
        mixed = jnp.concatenate([o_s[j, rows, :] for j in range(N_KV_HEADS)] + [y, mix_s[rows, :]], axis=1)
        out = jnp.dot(mixed, w_out_ref[...], preferred_element_type=F32)
        x1_ref[rows, :] = _layernorm(ALPHA * x_ref[rows, :] + g1 * out, ln_g_ref[...], ln_b_ref[...])


def _mixer(x2, mod, p, layer, *, batch, seq_len, n_seq, latent, ctx=None, rope=None, caches=None):
    B, L = batch, seq_len
    M = n_seq * L
    n_steps = B // n_seq
    T = L + (ctx[0].shape[2] if latent else 0)

    mod_row = (lambda i: i) if latent else (lambda i: mod.shape[1] - 1)
    in_specs = [pl.BlockSpec((M, D_MODEL), lambda i: (i, 0)),
                pl.BlockSpec((None, None, 1, 6 * D_MODEL), lambda i: (layer, mod_row(i), 0, 0))]
    args = [x2, mod]
    for name in ("w_in", "hsum", "gqk"):
        a = p[name]
        in_specs.append(_const_spec(a) if name == "hsum" else _layer_spec(a, layer))
        args.append(a)
    if latent:
        for a in rope:
            in_specs.append(_const_spec(a))
            args.append(a)
        ck, cv, st = ctx
        past = ck.shape[2]
        in_specs += [pl.BlockSpec((None, None, past, KV_W), lambda i: (i, layer, 0, 0)),
                     pl.BlockSpec((None, None, past, KV_W), lambda i: (i, layer, 0, 0)),
                     pl.BlockSpec((None, None, 2, LRU_W), lambda i: (i, layer, 0, 0))]
        args += [ck, cv, st]
    for name in ("conv_w", "conv_b", "wg", "bg", "lam", "mlp_g", "mlp_b", "ws", "bsb", "w_out", "ln1_g", "ln1_b"):
        in_specs.append(_layer_spec(p[name], layer))
        args.append(p[name])

    out_shape = [jax.ShapeDtypeStruct((B * L, D_MODEL), F32)]
    out_specs = [pl.BlockSpec((M, D_MODEL), lambda i: (i, 0))]
    aliases = {}
    if not latent:
        if caches is None:
            caches = (jnp.zeros((B, DEPTH, L, KV_W), F32), jnp.zeros((B, DEPTH, L, KV_W), F32),
                      jnp.zeros((B, DEPTH, 2, LRU_W), F32))
        for c in caches:
            aliases[len(args)] = len(out_shape)
            in_specs.append(pl.BlockSpec(memory_space=pl.ANY))
            args.append(c)
            out_shape.append(jax.ShapeDtypeStruct(c.shape, F32))
        out_specs += [pl.BlockSpec((n_seq, None, L, KV_W), lambda i: (i, layer, 0, 0)),
                      pl.BlockSpec((n_seq, None, L, KV_W), lambda i: (i, layer, 0, 0)),
                      pl.BlockSpec((n_seq, None, 2, LRU_W), lambda i: (i, layer, 0, 0))]

    qb = min(Q_BLOCK, L)
    kv_w = HEADS_PER_KV * HEAD_DIM
    scratch = [pltpu.VMEM((M, IN_W - XR0), F32),
               pltpu.VMEM((N_QGROUPS, M, LANES), BF16),
               pltpu.VMEM((n_seq, 2 * N_KV_HEADS, LANES, T), BF16),
               pltpu.VMEM((n_seq, N_KV_HEADS, HEADS_PER_KV, T, kv_w), BF16),
               pltpu.VMEM((N_KV_HEADS, M, kv_w), BF16),
               pltpu.VMEM((M, MLP_W), BF16),
               pltpu.VMEM((M, D_MODEL), BF16)]
    scratch += [pltpu.VMEM((HEADS_PER_KV, qb, T), BF16)] * 2
    scratch += [pltpu.VMEM((qb, kv_w), F32)] * 2
    scratch += [pltpu.VMEM((M, LRU_W), F32)] * 4

    return pl.pallas_call(
        functools.partial(_mixer_kernel, seq_len=L, n_seq=n_seq, latent=latent),
        grid=(n_steps,),
        in_specs=in_specs,
        out_specs=out_specs,
        out_shape=out_shape,
        scratch_shapes=scratch,
        input_output_aliases=aliases,
        compiler_params=pltpu.CompilerParams(dimension_semantics=("arbitrary",), vmem_limit_bytes=VMEM_LIMIT),
        name="mixer_latent" if latent else "mixer_context",
    )(*args)


def _ffn_kernel(x_ref, mod_ref, w1_ref, b1_ref, w2_ref, b2_ref, g_ref, b_ref, o_ref):
    mod = mod_ref[...]
    sh2, sc2, g2 = mod[:, 3 * D_MODEL:4 * D_MODEL], mod[:, 4 * D_MODEL:5 * D_MODEL], mod[:, 5 * D_MODEL:]
    for r in range(x_ref.shape[0] // FF_ROWS):
        rows = slice(r * FF_ROWS, (r + 1) * FF_ROWS)
        x = x_ref[rows, :]
        h = (x * (1.0 + sc2) + sh2).astype(BF16)
        acc = None
        for c in range(D_FF // FF_CHUNK):
            cols = slice(c * FF_CHUNK, (c + 1) * FF_CHUNK)
            t = jnp.dot(h, w1_ref[:, cols], preferred_element_type=F32) + b1_ref[:, cols]
            t = jnp.square(jnp.maximum(t, 0.0)).astype(BF16)
            o = jnp.dot(t, w2_ref[cols, :], preferred_element_type=F32)
            acc = o if acc is None else acc + o
        f = acc + b2_ref[...]
        o_ref[rows, :] = _layernorm(ALPHA * x + g2 * f, g_ref[...], b_ref[...])


def _ffn(x2, mod, p, layer, *, seq_len, latent, tm):
    rows = x2.shape[0]
    mod_row = (lambda i: (i * tm) // seq_len) if latent else (lambda i: mod.shape[1] - 1)
    in_specs = [pl.BlockSpec((tm, D_MODEL), lambda i: (i, 0)),
                pl.BlockSpec((None, None, 1, 6 * D_MODEL), lambda i: (layer, mod_row(i), 0, 0))]
    args = [x2, mod]
    for name in ("w_ff1", "b_ff1", "w_ff2", "b_ff2", "ln2_g", "ln2_b"):
        in_specs.append(_layer_spec(p[name], layer))
        args.append(p[name])
    return pl.pallas_call(
        _ffn_kernel,
        grid=(rows // tm,),
        in_specs=in_specs,
        out_specs=pl.BlockSpec((tm, D_MODEL), lambda i: (i, 0)),
        out_shape=jax.ShapeDtypeStruct((rows, D_MODEL), F32),
        compiler_params=pltpu.CompilerParams(dimension_semantics=("arbitrary",), vmem_limit_bytes=VMEM_LIMIT),
        name="ffn_latent" if latent else "ffn_context",
    )(*args)


def _rope_tables(length):
    rows = length // GRID_W
    pos_row = jnp.repeat(jnp.arange(rows), GRID_W).astype(F32)
    pos_col = jnp.tile(jnp.arange(GRID_W), rows).astype(F32)
    n_f = HEAD_DIM // 4
    inv = ROPE_THETA ** (-jnp.arange(n_f, dtype=F32) / n_f)
    ang = jnp.concatenate([pos_row[:, None] * inv, pos_col[:, None] * inv], -1)
    cos = jnp.repeat(jnp.cos(ang), 2, axis=-1)
    sin = jnp.repeat(jnp.sin(ang), 2, axis=-1)
    even = (jnp.arange(HEAD_DIM) % 2 == 0)[None, :]
    sin_a = jnp.where(even, -sin, 0.0)
    sin_b = jnp.where(even, 0.0, sin)
    rep = LANES // HEAD_DIM
    return tuple(jnp.tile(t, (1, rep)).astype(F32) for t in (cos, sin_a, sin_b))


def _block_diag(w):
    eye = jnp.eye(LRU_BLOCKS, dtype=w.dtype)
    full = w[..., :, :, None, :] * eye[:, None, :, None]
    return full.reshape(w.shape[:-3] + (LRU_W, LRU_W))


def _row(a):
    return a.reshape(a.shape[0], 1, a.shape[1])


def kernel(x_prompt, x_sample, c, cache_k, cache_v, state_lru, c_ctx, w_ada, b_ada, w_in, q_norm_g, k_norm_g,
           conv_w, conv_b, lru_wa, lru_ba, lru_wx, lru_bx, lru_lam, mlp_norm_g, mlp_norm_b, mlp_ws, mlp_bs,
           w_out, ln1_g, ln1_b, w_ff1, b_ff1, w_ff2, b_ff2, ln2_g, ln2_b):
    batch, seq, _ = x_prompt.shape
    dec_batch, dec_seq, _ = x_sample.shape
    past = cache_k.shape[2]

    n_cond = dec_batch + 1
    cond_rows = -(-n_cond // SUBLANES) * SUBLANES
    cond = jnp.concatenate([c, c_ctx[None, :], jnp.zeros((cond_rows - n_cond, D_MODEL), F32)], axis=0)
    mod = _modulation(cond, w_ada, b_ada)[:, :n_cond].reshape(DEPTH, n_cond, 1, 6 * D_MODEL)

    head_id = jnp.arange(QK_W) // HEAD_DIM
    wg = jnp.stack([_block_diag(lru_wa), _block_diag(lru_wx)], axis=2)
    wg = wg.transpose(0, 3, 1, 2, 4).reshape(DEPTH, LRU_W, 4 * LRU_W)
    p = dict(
        w_in=w_in.astype(BF16),
        hsum=(head_id[:, None] == head_id[None, :]).astype(BF16),
        gqk=_row(jnp.concatenate([jnp.tile(q_norm_g * Q_SCALE, (1, N_HEADS)), jnp.tile(k_norm_g, (1, N_KV_HEADS))],
                                 axis=1)),
        conv_w=conv_w, conv_b=_row(conv_b),
        wg=wg.astype(BF16),
        bg=jnp.stack([lru_ba, lru_bx], axis=2).reshape(DEPTH, 1, 4 * LRU_W),
        lam=lru_lam,
        mlp_g=_row(mlp_norm_g), mlp_b=_row(mlp_norm_b),
        ws=mlp_ws.reshape(DEPTH, MLP_GROUPS * CHUNK, CHUNK).astype(BF16),
        bsb=jnp.repeat(mlp_bs.transpose(0, 2, 1), MLP_GW, axis=2),
        w_out=w_out.astype(BF16), ln1_g=_row(ln1_g), ln1_b=_row(ln1_b),
        w_ff1=w_ff1.astype(BF16), b_ff1=_row(b_ff1), w_ff2=w_ff2.astype(BF16), b_ff2=_row(b_ff2),
        ln2_g=_row(ln2_g), ln2_b=_row(ln2_b),
    )
    rope = _rope_tables(dec_seq)
    ctx = (cache_k.reshape(dec_batch, DEPTH, past, KV_W), cache_v.reshape(dec_batch, DEPTH, past, KV_W), state_lru)

    y_prompt = x_prompt.reshape(batch * seq, D_MODEL)
    y_sample = x_sample.reshape(dec_batch * dec_seq, D_MODEL)
    caches = None
    for l in range(DEPTH):
        x1, *caches = _mixer(y_prompt, mod, p, l, batch=batch, seq_len=seq, n_seq=4, latent=False, caches=caches)
        y_prompt = _ffn(x1, mod, p, l, seq_len=seq, latent=False, tm=FF_TILE)
        (x1,) = _mixer(y_sample, mod, p, l, batch=dec_batch, seq_len=dec_seq, n_seq=1, latent=True,
                       ctx=ctx, rope=rope)
        y_sample = _ffn(x1, mod, p, l, seq_len=dec_seq, latent=True, tm=FF_TILE)

    new_k, new_v, new_s = caches
    return (y_prompt.reshape(batch, seq, D_MODEL), y_sample.reshape(dec_batch, dec_seq, D_MODEL),
            new_k.reshape(batch, DEPTH, seq, N_KV_HEADS, HEAD_DIM),
            new_v.reshape(batch, DEPTH, seq, N_KV_HEADS, HEAD_DIM), new_s)
```

```python
import functools
import math

import jax
import jax.numpy as jnp
from jax import lax
from jax.experimental import pallas as pl
from jax.experimental.pallas import tpu as pltpu

D_MODEL = 1024
DEPTH = 2
GRID_W = 64
CHUNK = 128
HEAD_DIM = 64
ATTN_W = 512
N_HEADS = 8
N_KV_HEADS = 2
KV_W = 128
LRU_W = 256
LRU_BLOCKS = 4
LRU_BW = 64
CONV_W = 4
RG_C = 8.0
MLP_W = 256
MLP_GROUPS = 4
MLP_GW = 64
MIX_W = 1024
IN_W = 1792
D_FF = 4096
ROPE_THETA = 10000.0
ALPHA = (2 * DEPTH) ** 0.25
EPS = 1e-6

Q0, K0, V0, XR0, GR0, ZM0 = 0, 512, 640, 768, 1024, 1280
QK_W = ATTN_W + KV_W
LANES = 128
SUBLANES = 8
N_QGROUPS = ATTN_W // LANES
HEADS_PER_KV = N_HEADS // N_KV_HEADS
Q_BLOCK = 256
FF_CHUNK = 1024
FF_TILE = 1024
FF_ROWS = 512
VMEM_LIMIT = 60 * 1024 * 1024
Q_SCALE = HEAD_DIM ** -0.5 * math.log2(math.e)

F32 = jnp.float32
BF16 = jnp.bfloat16


def _layernorm(x, g, b):
    mu = jnp.mean(x, -1, keepdims=True)
    xc = x - mu
    var = jnp.mean(xc * xc, -1, keepdims=True)
    return xc * lax.rsqrt(var + EPS) * g + b


def _gelu(x):
    return jax.nn.gelu(x, approximate=True)


def _layer_spec(arr, layer):
    n = arr.ndim - 1
    return pl.BlockSpec((None,) + arr.shape[1:], lambda *_: (layer,) + (0,) * n, pipeline_mode=pl.Buffered(1))


def _const_spec(arr):
    n = arr.ndim
    return pl.BlockSpec(arr.shape, lambda *_: (0,) * n, pipeline_mode=pl.Buffered(1))


def _mod_kernel(cond_ref, w_ref, b_ref, o_ref):
    cond = cond_ref[...]
    o_ref[...] = jnp.dot(jax.nn.silu(cond).astype(BF16), w_ref[...].astype(BF16),
                         preferred_element_type=F32) + b_ref[...]


def _modulation(cond, w_ada, b_ada):
    rows = cond.shape[0]
    tn = 1536
    return pl.pallas_call(
        _mod_kernel,
        grid=(DEPTH, 6 * D_MODEL // tn),
        in_specs=[pl.BlockSpec((rows, D_MODEL), lambda l, j: (0, 0)),
                  pl.BlockSpec((None, D_MODEL, tn), lambda l, j: (l, 0, j)),
                  pl.BlockSpec((None, 1, tn), lambda l, j: (l, 0, j))],
        out_specs=pl.BlockSpec((None, rows, tn), lambda l, j: (l, 0, j)),
        out_shape=jax.ShapeDtypeStruct((DEPTH, rows, 6 * D_MODEL), F32),
        compiler_params=pltpu.CompilerParams(dimension_semantics=("arbitrary", "arbitrary"),
                                             vmem_limit_bytes=VMEM_LIMIT),
        name="adaln_modulation",
    )(cond, w_ada, b_ada.reshape(DEPTH, 1, 6 * D_MODEL))


def _tile_scan(a, u, reverse):
    n_t, w = a.shape[0] // SUBLANES, a.shape[1]
    a3 = a.reshape(n_t, SUBLANES, w)
    u3 = u.reshape(n_t, SUBLANES, w)
    sub = lax.broadcasted_iota(jnp.int32, (1, SUBLANES, w), 1)
    d = 1
    while d < SUBLANES:
        keep = (sub < SUBLANES - d) if reverse else (sub >= d)
        shift = SUBLANES - d if reverse else d
        a_sh = pltpu.roll(a3, shift, 1)
        u_sh = pltpu.roll(u3, shift, 1)
        u3 = a3 * jnp.where(keep, u_sh, 0.0) + u3
        a3 = a3 * jnp.where(keep, a_sh, 1.0)
        d *= 2
    return a3.reshape(n_t * SUBLANES, w), u3.reshape(n_t * SUBLANES, w)


def _mixer_kernel(*refs, seq_len, n_seq, latent):
    it = iter(refs)
    x_ref, mod_ref, w_in_ref, hsum_ref, gqk_ref = next(it), next(it), next(it), next(it), next(it)
    if latent:
        cos_ref, sin_a_ref, sin_b_ref = next(it), next(it), next(it)
        ck_ref, cv_ref, st_ref = next(it), next(it), next(it)
    conv_w_ref, conv_b_ref, wg_ref, bg_ref, lam_ref = next(it), next(it), next(it), next(it), next(it)
    mlp_g_ref, mlp_b_ref, ws_ref, bsb_ref = next(it), next(it), next(it), next(it)
    w_out_ref, ln_g_ref, ln_b_ref = next(it), next(it), next(it)
    if not latent:
        next(it), next(it), next(it)
    x1_ref = next(it)
    if not latent:
        k_out_ref, v_out_ref, s_out_ref = next(it), next(it), next(it)
    z_s, q_s, kt_s, vp_s, o_s, mix_s = next(it), next(it), next(it), next(it), next(it), next(it)
    e_bufs, il_bufs = (next(it), next(it)), (next(it), next(it))
    af_s, uf_s, ab_s, ub_s = next(it), next(it), next(it), next(it)

    L = seq_len
    past = ck_ref.shape[0] if latent else 0
    T = past + L
    n_tiles = L // SUBLANES
    kv_w = HEADS_PER_KV * HEAD_DIM

    mod = mod_ref[...]
    sh1, sc1, g1 = mod[:, 0:D_MODEL], mod[:, D_MODEL:2 * D_MODEL], mod[:, 2 * D_MODEL:3 * D_MODEL]
    h = (x_ref[...] * (1.0 + sc1) + sh1).astype(BF16)
    zqkv = jnp.dot(h, w_in_ref[:, :XR0], preferred_element_type=F32)
    z_s[...] = jnp.dot(h, w_in_ref[:, XR0:], preferred_element_type=F32)

    qk = zqkv[:, Q0:Q0 + QK_W]
    ssq = jnp.dot((qk * qk).astype(BF16), hsum_ref[...], preferred_element_type=F32)
    qkn = qk * lax.rsqrt(ssq * (1.0 / HEAD_DIM) + EPS) * gqk_ref[...]

    lam = lam_ref[...]
    neg = -lam
    decay = -RG_C * (jnp.maximum(neg, 0.0) + jnp.log1p(jnp.exp(-jnp.abs(neg))))
    quarter = lax.broadcasted_iota(jnp.int32, (1, kv_w), 1) // HEAD_DIM

    for s in range(n_seq):
        r0 = s * L
        rows = slice(r0, r0 + L)
        qkn_s = qkn[rows]
        k_cur = qkn_s[:, ATTN_W:]
        v_cur = zqkv[rows, V0:V0 + KV_W]
        if latent:
            cos, sin_a, sin_b = cos_ref[...], sin_a_ref[...], sin_b_ref[...]

            def rope(t):
                return (t * cos + pltpu.roll(t, LANES - 1, 1) * sin_a + pltpu.roll(t, 1, 1) * sin_b)

            for g in range(N_QGROUPS):
                qg = rope(qkn_s[:, g * LANES:(g + 1) * LANES]) * Q_SCALE
                q_s[g, rows, :] = qg.astype(BF16)
            k_all = jnp.concatenate([ck_ref[...], rope(k_cur)], axis=0)
            v_all = jnp.concatenate([cv_ref[...], v_cur], axis=0)
        else:
            k_out_ref[s] = k_cur
            v_out_ref[s] = v_cur
            for g in range(N_QGROUPS):
                q_s[g, rows, :] = (qkn_s[:, g * LANES:(g + 1) * LANES] * Q_SCALE).astype(BF16)
            k_all, v_all = k_cur, v_cur

        kt = k_all.T
        zer = jnp.zeros((HEAD_DIM, T), F32)
        for j in range(N_KV_HEADS):
            kj = kt[j * HEAD_DIM:(j + 1) * HEAD_DIM]
            kt_s[s, 2 * j] = jnp.concatenate([kj, zer], axis=0).astype(BF16)
            kt_s[s, 2 * j + 1] = jnp.concatenate([zer, kj], axis=0).astype(BF16)
        v_rot = pltpu.roll(v_all, HEAD_DIM, 1)
        v_two = jnp.concatenate([v_all, v_all], axis=1)
        v_rot_two = jnp.concatenate([v_rot, v_rot], axis=1)
        for j in range(N_KV_HEADS):
            for hq in range(HEADS_PER_KV):
                src = v_two if hq % 2 == j else v_rot_two
                vp_s[s, j, hq] = jnp.where(quarter == hq, src, 0.0).astype(BF16)

    qb = min(Q_BLOCK, L)
    n_qb = L // qb
    n_units = n_seq * N_KV_HEADS * n_qb

    def unit_index(n):
        s = n // (N_KV_HEADS * n_qb)
        j = (n // n_qb) % N_KV_HEADS
        qrows = pl.ds(pl.multiple_of(s * L + (n % n_qb) * qb, qb), qb)
        return s, j, qrows

    def score_stage(n, e_ref, il_ref):
        s, j, qrows = unit_index(n)
        inv = None
        for hq in range(HEADS_PER_KV):
            sc = jnp.dot(q_s[2 * j + hq // 2, qrows, :], kt_s[s, 2 * j + hq % 2],
                         preferred_element_type=F32)
            e = jnp.exp2(sc - jnp.max(sc, -1, keepdims=True))
            e_ref[hq] = e.astype(BF16)
            il = 1.0 / jnp.sum(e, -1, keepdims=True)
            inv = il if inv is None else jnp.where(quarter >= hq, il, inv)
        il_ref[...] = inv

    def value_stage(n, e_ref, il_ref):
        s, j, qrows = unit_index(n)
        acc = None
        for hq in range(HEADS_PER_KV):
            o = jnp.dot(e_ref[hq], vp_s[s, j, hq], preferred_element_type=F32)
            acc = o if acc is None else acc + o
        o_s[j, qrows, :] = (acc * il_ref[...]).astype(BF16)

    def unit_pair(k, carry):
        score_stage(2 * k + 1, e_bufs[1], il_bufs[1])
        value_stage(2 * k, e_bufs[0], il_bufs[0])
        score_stage(2 * k + 2, e_bufs[0], il_bufs[0])
        value_stage(2 * k + 1, e_bufs[1], il_bufs[1])
        return carry

    score_stage(0, e_bufs[0], il_bufs[0])
    lax.fori_loop(0, n_units // 2 - 1, unit_pair, 0)
    score_stage(n_units - 1, e_bufs[1], il_bufs[1])
    value_stage(n_units - 2, e_bufs[0], il_bufs[0])
    value_stage(n_units - 1, e_bufs[1], il_bufs[1])

    for s in range(n_seq):
        r0 = s * L
        rows = slice(r0, r0 + L)
        xr = z_s[rows, 0:LRU_W]
        row = lax.broadcasted_iota(jnp.int32, (L, LRU_W), 0)
        cw = conv_w_ref[...]
        xc = (jnp.where(row >= 1, pltpu.roll(xr, 1, 0), 0.0) * cw[0:1]
              + xr * cw[1:2]
              + jnp.where(row < L - 1, pltpu.roll(xr, L - 1, 0), 0.0) * cw[2:3]
              + jnp.where(row < L - 2, pltpu.roll(xr, L - 2, 0), 0.0) * cw[3:4]
              + conv_b_ref[...])
        xcb = xc.astype(BF16)
        for d, (a_s, u_s) in enumerate(((af_s, uf_s), (ab_s, ub_s))):
            gates = jnp.dot(xcb, wg_ref[:, 2 * d * LRU_W:2 * (d + 1) * LRU_W],
                            preferred_element_type=F32) + bg_ref[:, 2 * d * LRU_W:2 * (d + 1) * LRU_W]
            r = jax.nn.sigmoid(gates[:, :LRU_W])
            i_g = jax.nn.sigmoid(gates[:, LRU_W:])
            log_a = decay[d:d + 1] * r
            a = jnp.exp(log_a)
            th = jnp.tanh(log_a)
            u = jnp.sqrt(-2.0 * th / (1.0 - th)) * i_g * xc
            a_t, u_t = _tile_scan(a, u, reverse=(d == 1))
            a_s[rows, :] = a_t
            u_s[rows, :] = u_t

        def tile_step(t, carry):
            cf, cb = carry
            rf = pl.ds(pl.multiple_of(r0 + t * SUBLANES, SUBLANES), SUBLANES)
            rb = pl.ds(pl.multiple_of(r0 + (n_tiles - 1 - t) * SUBLANES, SUBLANES), SUBLANES)
            hf = af_s[rf, :] * cf + uf_s[rf, :]
            hb = ab_s[rb, :] * cb + ub_s[rb, :]
            uf_s[rf, :] = hf
            ub_s[rb, :] = hb
            return (jnp.broadcast_to(hf[SUBLANES - 1:SUBLANES], (SUBLANES, LRU_W)),
                    jnp.broadcast_to(hb[0:1], (SUBLANES, LRU_W)))

        if latent:
            init = (jnp.broadcast_to(st_ref[0:1, :], (SUBLANES, LRU_W)),
                    jnp.broadcast_to(st_ref[1:2, :], (SUBLANES, LRU_W)))
        else:
            init = (jnp.zeros((SUBLANES, LRU_W), F32), jnp.zeros((SUBLANES, LRU_W), F32))
        fin_f, fin_b = lax.fori_loop(0, n_tiles, tile_step, init, unroll=4)
        if not latent:
            s_out_ref[s, 0:1, :] = fin_f[0:1]
            s_out_ref[s, 1:2, :] = fin_b[0:1]
        y = (uf_s[rows, :] + ub_s[rows, :]) * _gelu(z_s[rows, GR0 - XR0:GR0 - XR0 + LRU_W])
        mix_s[rows, 0:LRU_W] = y.astype(BF16)

        glane = lax.broadcasted_iota(jnp.int32, (CHUNK, MLP_W), 1) // MLP_GW
        for c in range(L // CHUNK):
            crow = slice(r0 + c * CHUNK, r0 + (c + 1) * CHUNK)
            zg = _gelu(z_s[crow, ZM0 - XR0:ZM0 - XR0 + 2 * MLP_W])
            u_m = zg[:, :MLP_W]
            vn = _layernorm(zg[:, MLP_W:], mlp_g_ref[...], mlp_b_ref[...]).astype(BF16)
            full = jnp.dot(ws_ref[...], vn, preferred_element_type=F32)
            sm = bsb_ref[...]
            for gi in range(MLP_GROUPS):
                sm = sm + jnp.where(glane == gi, full[gi * CHUNK:(gi + 1) * CHUNK], 0.0)
            mix_s[crow, LRU_W:] = (u_m * sm).astype(BF16)

    out = jnp.dot(mix_s[...], w_out_ref[ATTN_W:, :], preferred_element_type=F32)
    for j in range(N_KV_HEADS):
        out = out + jnp.dot(o_s[j], w_out_ref[j * kv_w:(j + 1) * kv_w, :], preferred_element_type=F32)
    x1_ref[...] = _layernorm(ALPHA * x_ref[...] + g1 * out, ln_g_ref[...], ln_b_ref[...])


def _mixer(x2, mod, p, layer, *, batch, seq_len, n_seq, latent, ctx=None, rope=None, caches=None):
    B, L = batch, seq_len
    M = n_seq * L
    n_steps = B // n_seq
    T = L + (ctx[0].shape[2] if latent else 0)

    mod_row = (lambda i: i) if latent else (lambda i: mod.shape[1] - 1)
    in_specs = [pl.BlockSpec((M, D_MODEL), lambda i: (i, 0)),
                pl.BlockSpec((None, None, 1, 6 * D_MODEL), lambda i: (layer, mod_row(i), 0, 0))]
    args = [x2, mod]
    for name in ("w_in", "hsum", "gqk"):
        a = p[name]
        in_specs.append(_const_spec(a) if name == "hsum" else _layer_spec(a, layer))
        args.append(a)
    if latent:
        for a in rope:
            in_specs.append(_const_spec(a))
            args.append(a)
        ck, cv, st = ctx
        past = ck.shape[2]
        in_specs += [pl.BlockSpec((None, None, past, KV_W), lambda i: (i, layer, 0, 0)),
                     pl.BlockSpec((None, None, past, KV_W), lambda i: (i, layer, 0, 0)),
                     pl.BlockSpec((None, None, 2, LRU_W), lambda i: (i, layer, 0, 0))]
        args += [ck, cv, st]
    for name in ("conv_w", "conv_b", "wg", "bg", "lam", "mlp_g", "mlp_b", "ws", "bsb", "w_out", "ln1_g", "ln1_b"):
        in_specs.append(_layer_spec(p[name], layer))
        args.append(p[name])

    out_shape = [jax.ShapeDtypeStruct((B * L, D_MODEL), F32)]
    out_specs = [pl.BlockSpec((M, D_MODEL), lambda i: (i, 0))]
    aliases = {}
    if not latent:
        if caches is None:
            caches = (jnp.zeros((B, DEPTH, L, KV_W), F32), jnp.zeros((B, DEPTH, L, KV_W), F32),
                      jnp.zeros((B, DEPTH, 2, LRU_W), F32))
        for c in caches:
            aliases[len(args)] = len(out_shape)
            in_specs.append(pl.BlockSpec(memory_space=pl.ANY))
            args.append(c)
            out_shape.append(jax.ShapeDtypeStruct(c.shape, F32))
        out_specs += [pl.BlockSpec((n_seq, None, L, KV_W), lambda i: (i, layer, 0, 0)),
                      pl.BlockSpec((n_seq, None, L, KV_W), lambda i: (i, layer, 0, 0)),
                      pl.BlockSpec((n_seq, None, 2, LRU_W), lambda i: (i, layer, 0, 0))]

    qb = min(Q_BLOCK, L)
    kv_w = HEADS_PER_KV * HEAD_DIM
    scratch = [pltpu.VMEM((M, IN_W - XR0), F32),
               pltpu.VMEM((N_QGROUPS, M, LANES), BF16),
               pltpu.VMEM((n_seq, 2 * N_KV_HEADS, LANES, T), BF16),
               pltpu.VMEM((n_seq, N_KV_HEADS, HEADS_PER_KV, T, kv_w), BF16),
               pltpu.VMEM((N_KV_HEADS, M, kv_w), BF16),
               pltpu.VMEM((M, LRU_W + MLP_W), BF16)]
    scratch += [pltpu.VMEM((HEADS_PER_KV, qb, T), BF16)] * 2
    scratch += [pltpu.VMEM((qb, kv_w), F32)] * 2
    scratch += [pltpu.VMEM((M, LRU_W), F32)] * 4

    return pl.pallas_call(
        functools.partial(_mixer_kernel, seq_len=L, n_seq=n_seq, latent=latent),
        grid=(n_steps,),
        in_specs=in_specs,
        out_specs=out_specs,
        out_shape=out_shape,
        scratch_shapes=scratch,
        input_output_aliases=aliases,
        compiler_params=pltpu.CompilerParams(dimension_semantics=("arbitrary",), vmem_limit_bytes=VMEM_LIMIT),
        name="mixer_latent" if latent else "mixer_context",
    )(*args)


def _ffn_kernel(x_ref, mod_ref, w1_ref, b1_ref, w2_ref, b2_ref, g_ref, b_ref, o_ref):
    mod = mod_ref[...]
    sh2, sc2, g2 = mod[:, 3 * D_MODEL:4 * D_MODEL], mod[:, 4 * D_MODEL:5 * D_MODEL], mod[:, 5 * D_MODEL:]
    for r in range(x_ref.shape[0] // FF_ROWS):
        rows = slice(r * FF_ROWS, (r + 1) * FF_ROWS)
        x = x_ref[rows, :]
        h = (x * (1.0 + sc2) + sh2).astype(BF16)
        acc = None
        for c in range(D_FF // FF_CHUNK):
            cols = slice(c * FF_CHUNK, (c + 1) * FF_CHUNK)
            t = jnp.dot(h, w1_ref[:, cols], preferred_element_type=F32) + b1_ref[:, cols]
            t = jnp.square(jnp.maximum(t, 0.0)).astype(BF16)
            o = jnp.dot(t, w2_ref[cols, :], preferred_element_type=F32)
            acc = o if acc is None else acc + o
        f = acc + b2_ref[...]
        o_ref[rows, :] = _layernorm(ALPHA * x + g2 * f, g_ref[...], b_ref[...])


def _ffn(x2, mod, p, layer, *, seq_len, latent, tm):
    rows = x2.shape[0]
    mod_row = (lambda i: (i * tm) // seq_len) if latent else (lambda i: mod.shape[1] - 1)
    in_specs = [pl.BlockSpec((tm, D_MODEL), lambda i: (i, 0)),
                pl.BlockSpec((None, None, 1, 6 * D_MODEL), lambda i: (layer, mod_row(i), 0, 0))]
    args = [x2, mod]
    for name in ("w_ff1", "b_ff1", "w_ff2", "b_ff2", "ln2_g", "ln2_b"):
        in_specs.append(_layer_spec(p[name], layer))
        args.append(p[name])
    return pl.pallas_call(
        _ffn_kernel,
        grid=(rows // tm,),
        in_specs=in_specs,
        out_specs=pl.BlockSpec((tm, D_MODEL), lambda i: (i, 0)),
        out_shape=jax.ShapeDtypeStruct((rows, D_MODEL), F32),
        compiler_params=pltpu.CompilerParams(dimension_semantics=("arbitrary",), vmem_limit_bytes=VMEM_LIMIT),
        name="ffn_latent" if latent else "ffn_context",
    )(*args)


def _rope_tables(length):
    rows = length // GRID_W
    pos_row = jnp.repeat(jnp.arange(rows), GRID_W).astype(F32)
    pos_col = jnp.tile(jnp.arange(GRID_W), rows).astype(F32)
    n_f = HEAD_DIM // 4
    inv = ROPE_THETA ** (-jnp.arange(n_f, dtype=F32) / n_f)
    ang = jnp.concatenate([pos_row[:, None] * inv, pos_col[:, None] * inv], -1)
    cos = jnp.repeat(jnp.cos(ang), 2, axis=-1)
    sin = jnp.repeat(jnp.sin(ang), 2, axis=-1)
    even = (jnp.arange(HEAD_DIM) % 2 == 0)[None, :]
    sin_a = jnp.where(even, -sin, 0.0)
    sin_b = jnp.where(even, 0.0, sin)
    rep = LANES // HEAD_DIM
    return tuple(jnp.tile(t, (1, rep)).astype(F32) for t in (cos, sin_a, sin_b))


def _block_diag(w):
    eye = jnp.eye(LRU_BLOCKS, dtype=w.dtype)
    full = w[..., :, :, None, :] * eye[:, None, :, None]
    return full.reshape(w.shape[:-3] + (LRU_W, LRU_W))


def _row(a):
    return a.reshape(a.shape[0], 1, a.shape[1])


def kernel(x_prompt, x_sample, c, cache_k, cache_v, state_lru, c_ctx, w_ada, b_ada, w_in, q_norm_g, k_norm_g,
           conv_w, conv_b, lru_wa, lru_ba, lru_wx, lru_bx, lru_lam, mlp_norm_g, mlp_norm_b, mlp_ws, mlp_bs,
           w_out, ln1_g, ln1_b, w_ff1, b_ff1, w_ff2, b_ff2, ln2_g, ln2_b):
    batch, seq, _ = x_prompt.shape
    dec_batch, dec_seq, _ = x_sample.shape
    past = cache_k.shape[2]

    n_cond = dec_batch + 1
    cond_rows = -(-n_cond // SUBLANES) * SUBLANES
    cond = jnp.concatenate([c, c_ctx[None, :], jnp.zeros((cond_rows - n_cond, D_MODEL), F32)], axis=0)
    mod = _modulation(cond, w_ada, b_ada)[:, :n_cond].reshape(DEPTH, n_cond, 1, 6 * D_MODEL)

    head_id = jnp.arange(QK_W) // HEAD_DIM
    wg = jnp.stack([_block_diag(lru_wa), _block_diag(lru_wx)], axis=2)
    wg = wg.transpose(0, 3, 1, 2, 4).reshape(DEPTH, LRU_W, 4 * LRU_W)
    p = dict(
        w_in=w_in.astype(BF16),
        hsum=(head_id[:, None] == head_id[None, :]).astype(BF16),
        gqk=_row(jnp.concatenate([jnp.tile(q_norm_g, (1, N_HEADS)), jnp.tile(k_norm_g, (1, N_KV_HEADS))], axis=1)),
        conv_w=conv_w, conv_b=_row(conv_b),
        wg=wg.astype(BF16),
        bg=jnp.stack([lru_ba, lru_bx], axis=2).reshape(DEPTH, 1, 4 * LRU_W),
        lam=lru_lam,
        mlp_g=_row(mlp_norm_g), mlp_b=_row(mlp_norm_b),
        ws=mlp_ws.reshape(DEPTH, MLP_GROUPS * CHUNK, CHUNK).astype(BF16),
        bsb=jnp.repeat(mlp_bs.transpose(0, 2, 1), MLP_GW, axis=2),
        w_out=w_out.astype(BF16), ln1_g=_row(ln1_g), ln1_b=_row(ln1_b),
        w_ff1=w_ff1.astype(BF16), b_ff1=_row(b_ff1), w_ff2=w_ff2.astype(BF16), b_ff2=_row(b_ff2),
        ln2_g=_row(ln2_g), ln2_b=_row(ln2_b),
    )
    rope = _rope_tables(dec_seq)
    ctx = (cache_k.reshape(dec_batch, DEPTH, past, KV_W), cache_v.reshape(dec_batch, DEPTH, past, KV_W), state_lru)

    y_prompt = x_prompt.reshape(batch * seq, D_MODEL)
    y_sample = x_sample.reshape(dec_batch * dec_seq, D_MODEL)
    caches = None
    for l in range(DEPTH):
        x1, *caches = _mixer(y_prompt, mod, p, l, batch=batch, seq_len=seq, n_seq=2, latent=False, caches=caches)
        y_prompt = _ffn(x1, mod, p, l, seq_len=seq, latent=False, tm=FF_TILE)
        (x1,) = _mixer(y_sample, mod, p, l, batch=dec_batch, seq_len=dec_seq, n_seq=1, latent=True,
                       ctx=ctx, rope=rope)
        y_sample = _ffn(x1, mod, p, l, seq_len=dec_seq, latent=True, tm=FF_TILE)

    new_k, new_v, new_s = caches
    return (y_prompt.reshape(batch, seq, D_MODEL), y_sample.reshape(dec_batch, dec_seq, D_MODEL),
            new_k.reshape(batch, DEPTH, seq, N_KV_HEADS, HEAD_DIM),
            new_v.reshape(batch, DEPTH, seq, N_KV_HEADS, HEAD_DIM), new_s)
```

```python
import functools
import math

import jax
import jax.numpy as jnp
from jax import lax
from jax.experimental import pallas as pl
from jax.experimental.pallas import tpu as pltpu

D_MODEL = 1024
DEPTH = 2
GRID_W = 64
CHUNK = 128
HEAD_DIM = 64
ATTN_W = 512
N_HEADS = 8
N_KV_HEADS = 2
KV_W = 128
LRU_W = 256
LRU_BLOCKS = 4
LRU_BW = 64
CONV_W = 4
RG_C = 8.0
MLP_W = 256
MLP_GROUPS = 4
MLP_GW = 64
MIX_W = 1024
IN_W = 1792
D_FF = 4096
ROPE_THETA = 10000.0
ALPHA = (2 * DEPTH) ** 0.25
EPS = 1e-6

Q0, K0, V0, XR0, GR0, ZM0 = 0, 512, 640, 768, 1024, 1280
QK_W = ATTN_W + KV_W
LANES = 128
SUBLANES = 8
N_QGROUPS = ATTN_W // LANES
HEADS_PER_KV = N_HEADS // N_KV_HEADS
Q_BLOCK = 256
FF_CHUNK = 1024
FF_TILE = 1024
FF_ROWS = 512
VMEM_LIMIT = 60 * 1024 * 1024
Q_SCALE = HEAD_DIM ** -0.5 * math.log2(math.e)

F32 = jnp.float32
BF16 = jnp.bfloat16


def _layernorm(x, g, b):
    mu = jnp.mean(x, -1, keepdims=True)
    xc = x - mu
    var = jnp.mean(xc * xc, -1, keepdims=True)
    return xc * lax.rsqrt(var + EPS) * g + b


def _gelu(x):
    return jax.nn.gelu(x, approximate=True)


def _layer_spec(arr, layer):
    n = arr.ndim - 1
    return pl.BlockSpec((None,) + arr.shape[1:], lambda *_: (layer,) + (0,) * n, pipeline_mode=pl.Buffered(1))


def _const_spec(arr):
    n = arr.ndim
    return pl.BlockSpec(arr.shape, lambda *_: (0,) * n, pipeline_mode=pl.Buffered(1))


def _mod_kernel(cond_ref, w_ref, b_ref, o_ref):
    cond = cond_ref[...]
    o_ref[...] = jnp.dot(jax.nn.silu(cond).astype(BF16), w_ref[...].astype(BF16),
                         preferred_element_type=F32) + b_ref[...]


def _modulation(cond, w_ada, b_ada):
    rows = cond.shape[0]
    tn = 1536
    return pl.pallas_call(
        _mod_kernel,
        grid=(DEPTH, 6 * D_MODEL // tn),
        in_specs=[pl.BlockSpec((rows, D_MODEL), lambda l, j: (0, 0)),
                  pl.BlockSpec((None, D_MODEL, tn), lambda l, j: (l, 0, j)),
                  pl.BlockSpec((None, 1, tn), lambda l, j: (l, 0, j))],
        out_specs=pl.BlockSpec((None, rows, tn), lambda l, j: (l, 0, j)),
        out_shape=jax.ShapeDtypeStruct((DEPTH, rows, 6 * D_MODEL), F32),
        compiler_params=pltpu.CompilerParams(dimension_semantics=("arbitrary", "arbitrary"),
                                             vmem_limit_bytes=VMEM_LIMIT),
        name="adaln_modulation",
    )(cond, w_ada, b_ada.reshape(DEPTH, 1, 6 * D_MODEL))


def _tile_scan(a, u, reverse):
    n_t, w = a.shape[0] // SUBLANES, a.shape[1]
    a3 = a.reshape(n_t, SUBLANES, w)
    u3 = u.reshape(n_t, SUBLANES, w)
    sub = lax.broadcasted_iota(jnp.int32, (1, SUBLANES, w), 1)
    d = 1
    while d < SUBLANES:
        keep = (sub < SUBLANES - d) if reverse else (sub >= d)
        shift = SUBLANES - d if reverse else d
        a_sh = pltpu.roll(a3, shift, 1)
        u_sh = pltpu.roll(u3, shift, 1)
        u3 = a3 * jnp.where(keep, u_sh, 0.0) + u3
        a3 = a3 * jnp.where(keep, a_sh, 1.0)
        d *= 2
    return a3.reshape(n_t * SUBLANES, w), u3.reshape(n_t * SUBLANES, w)


def _mixer_kernel(*refs, seq_len, n_seq, latent):
    it = iter(refs)
    x_ref, mod_ref, w_in_ref, hsum_ref, gqk_ref = next(it), next(it), next(it), next(it), next(it)
    if latent:
        cos_ref, sin_a_ref, sin_b_ref = next(it), next(it), next(it)
        ck_ref, cv_ref, st_ref = next(it), next(it), next(it)
    conv_w_ref, conv_b_ref, wg_ref, bg_ref, lam_ref = next(it), next(it), next(it), next(it), next(it)
    mlp_g_ref, mlp_b_ref, ws_ref, bsb_ref = next(it), next(it), next(it), next(it)
    w_out_ref, ln_g_ref, ln_b_ref = next(it), next(it), next(it)
    if not latent:
        next(it), next(it), next(it)
    x1_ref = next(it)
    if not latent:
        k_out_ref, v_out_ref, s_out_ref = next(it), next(it), next(it)
    z_s, q_s, kt_s, vp_s, o_s, mix_s = next(it), next(it), next(it), next(it), next(it), next(it)
    e_bufs, il_bufs = (next(it), next(it)), (next(it), next(it))
    af_s, uf_s, ab_s, ub_s = next(it), next(it), next(it), next(it)

    L = seq_len
    past = ck_ref.shape[0] if latent else 0
    T = past + L
    n_tiles = L // SUBLANES
    kv_w = HEADS_PER_KV * HEAD_DIM

    mod = mod_ref[...]
    sh1, sc1, g1 = mod[:, 0:D_MODEL], mod[:, D_MODEL:2 * D_MODEL], mod[:, 2 * D_MODEL:3 * D_MODEL]
    h = (x_ref[...] * (1.0 + sc1) + sh1).astype(BF16)
    zqkv = jnp.dot(h, w_in_ref[:, :XR0], preferred_element_type=F32)
    z_s[...] = jnp.dot(h, w_in_ref[:, XR0:], preferred_element_type=F32)

    qk = zqkv[:, Q0:Q0 + QK_W]
    ssq = jnp.dot((qk * qk).astype(BF16), hsum_ref[...], preferred_element_type=F32)
    qkn = qk * lax.rsqrt(ssq * (1.0 / HEAD_DIM) + EPS) * gqk_ref[...]

    lam = lam_ref[...]
    neg = -lam
    decay = -RG_C * (jnp.maximum(neg, 0.0) + jnp.log1p(jnp.exp(-jnp.abs(neg))))
    quarter = lax.broadcasted_iota(jnp.int32, (1, kv_w), 1) // HEAD_DIM

    for s in range(n_seq):
        r0 = s * L
        rows = slice(r0, r0 + L)
        qkn_s = qkn[rows]
        k_cur = qkn_s[:, ATTN_W:]
        v_cur = zqkv[rows, V0:V0 + KV_W]
        if latent:
            cos, sin_a, sin_b = cos_ref[...], sin_a_ref[...], sin_b_ref[...]

            def rope(t):
                return (t * cos + pltpu.roll(t, LANES - 1, 1) * sin_a + pltpu.roll(t, 1, 1) * sin_b)

            for g in range(N_QGROUPS):
                qg = rope(qkn_s[:, g * LANES:(g + 1) * LANES]) * Q_SCALE
                q_s[g, rows, :] = qg.astype(BF16)
            k_all = jnp.concatenate([ck_ref[...], rope(k_cur)], axis=0)
            v_all = jnp.concatenate([cv_ref[...], v_cur], axis=0)
        else:
            k_out_ref[s] = k_cur
            v_out_ref[s] = v_cur
            for g in range(N_QGROUPS):
                q_s[g, rows, :] = (qkn_s[:, g * LANES:(g + 1) * LANES] * Q_SCALE).astype(BF16)
            k_all, v_all = k_cur, v_cur

        kt = k_all.T
        zer = jnp.zeros((HEAD_DIM, T), F32)
        for j in range(N_KV_HEADS):
            kj = kt[j * HEAD_DIM:(j + 1) * HEAD_DIM]
            kt_s[s, 2 * j] = jnp.concatenate([kj, zer], axis=0).astype(BF16)
            kt_s[s, 2 * j + 1] = jnp.concatenate([zer, kj], axis=0).astype(BF16)
        v_rot = pltpu.roll(v_all, HEAD_DIM, 1)
        v_two = jnp.concatenate([v_all, v_all], axis=1)
        v_rot_two = jnp.concatenate([v_rot, v_rot], axis=1)
        for j in range(N_KV_HEADS):
            for hq in range(HEADS_PER_KV):
                src = v_two if hq % 2 == j else v_rot_two
                vp_s[s, j, hq] = jnp.where(quarter == hq, src, 0.0).astype(BF16)

    qb = min(Q_BLOCK, L)
    n_qb = L // qb
    n_units = n_seq * N_KV_HEADS * n_qb

    def unit_index(n):
        s = n // (N_KV_HEADS * n_qb)
        j = (n // n_qb) % N_KV_HEADS
        qrows = pl.ds(pl.multiple_of(s * L + (n % n_qb) * qb, qb), qb)
        return s, j, qrows

    def score_stage(n, e_ref, il_ref):
        s, j, qrows = unit_index(n)
        inv = None
        for hq in range(HEADS_PER_KV):
            sc = jnp.dot(q_s[2 * j + hq // 2, qrows, :], kt_s[s, 2 * j + hq % 2],
                         preferred_element_type=F32)
            e = jnp.exp2(sc - jnp.max(sc, -1, keepdims=True))
            e_ref[hq] = e.astype(BF16)
            il = 1.0 / jnp.sum(e, -1, keepdims=True)
            inv = il if inv is None else jnp.where(quarter >= hq, il, inv)
        il_ref[...] = inv

    def value_stage(n, e_ref, il_ref):
        s, j, qrows = unit_index(n)
        acc = None
        for hq in range(HEADS_PER_KV):
            o = jnp.dot(e_ref[hq], vp_s[s, j, hq], preferred_element_type=F32)
            acc = o if acc is None else acc + o
        o_s[j, qrows, :] = (acc * il_ref[...]).astype(BF16)

    def lru_chunk(c):
        r0 = pl.multiple_of(c * qb, qb)
        pos = (c % n_qb) * qb
        xr = z_s[pl.ds(r0, qb), 0:LRU_W]
        row = lax.broadcasted_iota(jnp.int32, (qb, LRU_W), 0)
        if n_qb == 1:
            before = after1 = after2 = jnp.zeros((1, LRU_W), F32)
        else:
            prev = z_s[pl.ds(pl.multiple_of(jnp.maximum(r0 - SUBLANES, 0), SUBLANES), SUBLANES), 0:LRU_W]
            nxt = z_s[pl.ds(pl.multiple_of(jnp.minimum(r0 + qb, n_seq * L - SUBLANES), SUBLANES), SUBLANES), 0:LRU_W]
            before = jnp.where(pos > 0, prev[SUBLANES - 1:SUBLANES], 0.0)
            after1 = jnp.where(pos + qb < L, nxt[0:1], 0.0)
            after2 = jnp.where(pos + qb < L, nxt[1:2], 0.0)
        x_m1 = jnp.where(row >= 1, pltpu.roll(xr, 1, 0), before)
        x_p1 = jnp.where(row < qb - 1, pltpu.roll(xr, qb - 1, 0), after1)
        x_p2 = jnp.where(row < qb - 2, pltpu.roll(xr, qb - 2, 0), jnp.where(row == qb - 2, after1, after2))
        cw = conv_w_ref[...]
        xc = x_m1 * cw[0:1] + xr * cw[1:2] + x_p1 * cw[2:3] + x_p2 * cw[3:4] + conv_b_ref[...]
        xcb = xc.astype(BF16)
        for d, (a_s, u_s) in enumerate(((af_s, uf_s), (ab_s, ub_s))):
            gates = jnp.dot(xcb, wg_ref[:, 2 * d * LRU_W:2 * (d + 1) * LRU_W],
                            preferred_element_type=F32) + bg_ref[:, 2 * d * LRU_W:2 * (d + 1) * LRU_W]
            r = jax.nn.sigmoid(gates[:, :LRU_W])
            i_g = jax.nn.sigmoid(gates[:, LRU_W:])
            log_a = decay[d:d + 1] * r
            a = jnp.exp(log_a)
            th = jnp.tanh(log_a)
            u = jnp.sqrt(-2.0 * th / (1.0 - th)) * i_g * xc
            a_t, u_t = _tile_scan(a, u, reverse=(d == 1))
            a_s[pl.ds(r0, qb), :] = a_t
            u_s[pl.ds(r0, qb), :] = u_t

    def unit_pair(k, carry):
        lru_chunk(k + 1)
        score_stage(2 * k + 1, e_bufs[1], il_bufs[1])
        value_stage(2 * k, e_bufs[0], il_bufs[0])
        score_stage(2 * k + 2, e_bufs[0], il_bufs[0])
        value_stage(2 * k + 1, e_bufs[1], il_bufs[1])
        return carry

    score_stage(0, e_bufs[0], il_bufs[0])
    lru_chunk(0)
    lax.fori_loop(0, n_units // 2 - 1, unit_pair, 0)
    score_stage(n_units - 1, e_bufs[1], il_bufs[1])
    value_stage(n_units - 2, e_bufs[0], il_bufs[0])
    value_stage(n_units - 1, e_bufs[1], il_bufs[1])

    for s in range(n_seq):
        r0 = s * L
        rows = slice(r0, r0 + L)
        def tile_step(t, carry):
            cf, cb = carry
            rf = pl.ds(pl.multiple_of(r0 + t * SUBLANES, SUBLANES), SUBLANES)
            rb = pl.ds(pl.multiple_of(r0 + (n_tiles - 1 - t) * SUBLANES, SUBLANES), SUBLANES)
            hf = af_s[rf, :] * cf + uf_s[rf, :]
            hb = ab_s[rb, :] * cb + ub_s[rb, :]
            uf_s[rf, :] = hf
            ub_s[rb, :] = hb
            return (jnp.broadcast_to(hf[SUBLANES - 1:SUBLANES], (SUBLANES, LRU_W)),
                    jnp.broadcast_to(hb[0:1], (SUBLANES, LRU_W)))

        if latent:
            init = (jnp.broadcast_to(st_ref[0:1, :], (SUBLANES, LRU_W)),
                    jnp.broadcast_to(st_ref[1:2, :], (SUBLANES, LRU_W)))
        else:
            init = (jnp.zeros((SUBLANES, LRU_W), F32), jnp.zeros((SUBLANES, LRU_W), F32))
        fin_f, fin_b = lax.fori_loop(0, n_tiles, tile_step, init, unroll=4)
        if not latent:
            s_out_ref[s, 0:1, :] = fin_f[0:1]
            s_out_ref[s, 1:2, :] = fin_b[0:1]
        y = (uf_s[rows, :] + ub_s[rows, :]) * _gelu(z_s[rows, GR0 - XR0:GR0 - XR0 + LRU_W])
        mix_s[rows, 0:LRU_W] = y.astype(BF16)

        glane = lax.broadcasted_iota(jnp.int32, (CHUNK, MLP_W), 1) // MLP_GW
        for c in range(L // CHUNK):
            crow = slice(r0 + c * CHUNK, r0 + (c + 1) * CHUNK)
            zg = _gelu(z_s[crow, ZM0 - XR0:ZM0 - XR0 + 2 * MLP_W])
            u_m = zg[:, :MLP_W]
            vn = _layernorm(zg[:, MLP_W:], mlp_g_ref[...], mlp_b_ref[...]).astype(BF16)
            full = jnp.dot(ws_ref[...], vn, preferred_element_type=F32)
            sm = bsb_ref[...]
            for gi in range(MLP_GROUPS):
                sm = sm + jnp.where(glane == gi, full[gi * CHUNK:(gi + 1) * CHUNK], 0.0)
            mix_s[crow, LRU_W:] = (u_m * sm).astype(BF16)

    out = jnp.dot(mix_s[...], w_out_ref[ATTN_W:, :], preferred_element_type=F32)
    for j in range(N_KV_HEADS):
        out = out + jnp.dot(o_s[j], w_out_ref[j * kv_w:(j + 1) * kv_w, :], preferred_element_type=F32)
    x1_ref[...] = _layernorm(ALPHA * x_ref[...] + g1 * out, ln_g_ref[...], ln_b_ref[...])


def _mixer(x2, mod, p, layer, *, batch, seq_len, n_seq, latent, ctx=None, rope=None, caches=None):
    B, L = batch, seq_len
    M = n_seq * L
    n_steps = B // n_seq
    T = L + (ctx[0].shape[2] if latent else 0)

    mod_row = (lambda i: i) if latent else (lambda i: mod.shape[1] - 1)
    in_specs = [pl.BlockSpec((M, D_MODEL), lambda i: (i, 0)),
                pl.BlockSpec((None, None, 1, 6 * D_MODEL), lambda i: (layer, mod_row(i), 0, 0))]
    args = [x2, mod]
    for name in ("w_in", "hsum", "gqk"):
        a = p[name]
        in_specs.append(_const_spec(a) if name == "hsum" else _layer_spec(a, layer))
        args.append(a)
    if latent:
        for a in rope:
            in_specs.append(_const_spec(a))
            args.append(a)
        ck, cv, st = ctx
        past = ck.shape[2]
        in_specs += [pl.BlockSpec((None, None, past, KV_W), lambda i: (i, layer, 0, 0)),
                     pl.BlockSpec((None, None, past, KV_W), lambda i: (i, layer, 0, 0)),
                     pl.BlockSpec((None, None, 2, LRU_W), lambda i: (i, layer, 0, 0))]
        args += [ck, cv, st]
    for name in ("conv_w", "conv_b", "wg", "bg", "lam", "mlp_g", "mlp_b", "ws", "bsb", "w_out", "ln1_g", "ln1_b"):
        in_specs.append(_layer_spec(p[name], layer))
        args.append(p[name])

    out_shape = [jax.ShapeDtypeStruct((B * L, D_MODEL), F32)]
    out_specs = [pl.BlockSpec((M, D_MODEL), lambda i: (i, 0))]
    aliases = {}
    if not latent:
        if caches is None:
            caches = (jnp.zeros((B, DEPTH, L, KV_W), F32), jnp.zeros((B, DEPTH, L, KV_W), F32),
                      jnp.zeros((B, DEPTH, 2, LRU_W), F32))
        for c in caches:
            aliases[len(args)] = len(out_shape)
            in_specs.append(pl.BlockSpec(memory_space=pl.ANY))
            args.append(c)
            out_shape.append(jax.ShapeDtypeStruct(c.shape, F32))
        out_specs += [pl.BlockSpec((n_seq, None, L, KV_W), lambda i: (i, layer, 0, 0)),
                      pl.BlockSpec((n_seq, None, L, KV_W), lambda i: (i, layer, 0, 0)),
                      pl.BlockSpec((n_seq, None, 2, LRU_W), lambda i: (i, layer, 0, 0))]

    qb = min(Q_BLOCK, L)
    kv_w = HEADS_PER_KV * HEAD_DIM
    scratch = [pltpu.VMEM((M, IN_W - XR0), F32),
               pltpu.VMEM((N_QGROUPS, M, LANES), BF16),
               pltpu.VMEM((n_seq, 2 * N_KV_HEADS, LANES, T), BF16),
               pltpu.VMEM((n_seq, N_KV_HEADS, HEADS_PER_KV, T, kv_w), BF16),
               pltpu.VMEM((N_KV_HEADS, M, kv_w), BF16),
               pltpu.VMEM((M, LRU_W + MLP_W), BF16)]
    scratch += [pltpu.VMEM((HEADS_PER_KV, qb, T), BF16)] * 2
    scratch += [pltpu.VMEM((qb, kv_w), F32)] * 2
    scratch += [pltpu.VMEM((M, LRU_W), F32)] * 4

    return pl.pallas_call(
        functools.partial(_mixer_kernel, seq_len=L, n_seq=n_seq, latent=latent),
        grid=(n_steps,),
        in_specs=in_specs,
        out_specs=out_specs,
        out_shape=out_shape,
        scratch_shapes=scratch,
        input_output_aliases=aliases,
        compiler_params=pltpu.CompilerParams(dimension_semantics=("arbitrary",), vmem_limit_bytes=VMEM_LIMIT),
        name="mixer_latent" if latent else "mixer_context",
    )(*args)


def _ffn_kernel(x_ref, mod_ref, w1_ref, b1_ref, w2_ref, b2_ref, g_ref, b_ref, o_ref):
    mod = mod_ref[...]
    sh2, sc2, g2 = mod[:, 3 * D_MODEL:4 * D_MODEL], mod[:, 4 * D_MODEL:5 * D_MODEL], mod[:, 5 * D_MODEL:]
    for r in range(x_ref.shape[0] // FF_ROWS):
        rows = slice(r * FF_ROWS, (r + 1) * FF_ROWS)
        x = x_ref[rows, :]
        h = (x * (1.0 + sc2) + sh2).astype(BF16)
        acc = None
        for c in range(D_FF // FF_CHUNK):
            cols = slice(c * FF_CHUNK, (c + 1) * FF_CHUNK)
            t = jnp.dot(h, w1_ref[:, cols], preferred_element_type=F32) + b1_ref[:, cols]
            t = jnp.square(jnp.maximum(t, 0.0)).astype(BF16)
            o = jnp.dot(t, w2_ref[cols, :], preferred_element_type=F32)
            acc = o if acc is None else acc + o
        f = acc + b2_ref[...]
        o_ref[rows, :] = _layernorm(ALPHA * x + g2 * f, g_ref[...], b_ref[...])


def _ffn(x2, mod, p, layer, *, seq_len, latent, tm):
    rows = x2.shape[0]
    mod_row = (lambda i: (i * tm) // seq_len) if latent else (lambda i: mod.shape[1] - 1)
    in_specs = [pl.BlockSpec((tm, D_MODEL), lambda i: (i, 0)),
                pl.BlockSpec((None, None, 1, 6 * D_MODEL), lambda i: (layer, mod_row(i), 0, 0))]
    args = [x2, mod]
    for name in ("w_ff1", "b_ff1", "w_ff2", "b_ff2", "ln2_g", "ln2_b"):
        in_specs.append(_layer_spec(p[name], layer))
        args.append(p[name])
    return pl.pallas_call(
        _ffn_kernel,
        grid=(rows // tm,),
        in_specs=in_specs,
        out_specs=pl.BlockSpec((tm, D_MODEL), lambda i: (i, 0)),
        out_shape=jax.ShapeDtypeStruct((rows, D_MODEL), F32),
        compiler_params=pltpu.CompilerParams(dimension_semantics=("arbitrary",), vmem_limit_bytes=VMEM_LIMIT),
        name="ffn_latent" if latent else "ffn_context",
    )(*args)


def _rope_tables(length):
    rows = length // GRID_W
    pos_row = jnp.repeat(jnp.arange(rows), GRID_W).astype(F32)
    pos_col = jnp.tile(jnp.arange(GRID_W), rows).astype(F32)
    n_f = HEAD_DIM // 4
    inv = ROPE_THETA ** (-jnp.arange(n_f, dtype=F32) / n_f)
    ang = jnp.concatenate([pos_row[:, None] * inv, pos_col[:, None] * inv], -1)
    cos = jnp.repeat(jnp.cos(ang), 2, axis=-1)
    sin = jnp.repeat(jnp.sin(ang), 2, axis=-1)
    even = (jnp.arange(HEAD_DIM) % 2 == 0)[None, :]
    sin_a = jnp.where(even, -sin, 0.0)
    sin_b = jnp.where(even, 0.0, sin)
    rep = LANES // HEAD_DIM
    return tuple(jnp.tile(t, (1, rep)).astype(F32) for t in (cos, sin_a, sin_b))


def _block_diag(w):
    eye = jnp.eye(LRU_BLOCKS, dtype=w.dtype)
    full = w[..., :, :, None, :] * eye[:, None, :, None]
    return full.reshape(w.shape[:-3] + (LRU_W, LRU_W))


def _row(a):
    return a.reshape(a.shape[0], 1, a.shape[1])


def kernel(x_prompt, x_sample, c, cache_k, cache_v, state_lru, c_ctx, w_ada, b_ada, w_in, q_norm_g, k_norm_g,
           conv_w, conv_b, lru_wa, lru_ba, lru_wx, lru_bx, lru_lam, mlp_norm_g, mlp_norm_b, mlp_ws, mlp_bs,
           w_out, ln1_g, ln1_b, w_ff1, b_ff1, w_ff2, b_ff2, ln2_g, ln2_b):
    batch, seq, _ = x_prompt.shape
    dec_batch, dec_seq, _ = x_sample.shape
    past = cache_k.shape[2]

    n_cond = dec_batch + 1
    cond_rows = -(-n_cond // SUBLANES) * SUBLANES
    cond = jnp.concatenate([c, c_ctx[None, :], jnp.zeros((cond_rows - n_cond, D_MODEL), F32)], axis=0)
    mod = _modulation(cond, w_ada, b_ada)[:, :n_cond].reshape(DEPTH, n_cond, 1, 6 * D_MODEL)

    head_id = jnp.arange(QK_W) // HEAD_DIM
    wg = jnp.stack([_block_diag(lru_wa), _block_diag(lru_wx)], axis=2)
    wg = wg.transpose(0, 3, 1, 2, 4).reshape(DEPTH, LRU_W, 4 * LRU_W)
    p = dict(
        w_in=w_in.astype(BF16),
        hsum=(head_id[:, None] == head_id[None, :]).astype(BF16),
        gqk=_row(jnp.concatenate([jnp.tile(q_norm_g, (1, N_HEADS)), jnp.tile(k_norm_g, (1, N_KV_HEADS))], axis=1)),
        conv_w=conv_w, conv_b=_row(conv_b),
        wg=wg.astype(BF16),
        bg=jnp.stack([lru_ba, lru_bx], axis=2).reshape(DEPTH, 1, 4 * LRU_W),
        lam=lru_lam,
        mlp_g=_row(mlp_norm_g), mlp_b=_row(mlp_norm_b),
        ws=mlp_ws.reshape(DEPTH, MLP_GROUPS * CHUNK, CHUNK).astype(BF16),
        bsb=jnp.repeat(mlp_bs.transpose(0, 2, 1), MLP_GW, axis=2),
        w_out=w_out.astype(BF16), ln1_g=_row(ln1_g), ln1_b=_row(ln1_b),
        w_ff1=w_ff1.astype(BF16), b_ff1=_row(b_ff1), w_ff2=w_ff2.astype(BF16), b_ff2=_row(b_ff2),
        ln2_g=_row(ln2_g), ln2_b=_row(ln2_b),
    )
    rope = _rope_tables(dec_seq)
    ctx = (cache_k.reshape(dec_batch, DEPTH, past, KV_W), cache_v.reshape(dec_batch, DEPTH, past, KV_W), state_lru)

    y_prompt = x_prompt.reshape(batch * seq, D_MODEL)
    y_sample = x_sample.reshape(dec_batch * dec_seq, D_MODEL)
    caches = None
    for l in range(DEPTH):
        x1, *caches = _mixer(y_prompt, mod, p, l, batch=batch, seq_len=seq, n_seq=2, latent=False, caches=caches)
        y_prompt = _ffn(x1, mod, p, l, seq_len=seq, latent=False, tm=FF_TILE)
        (x1,) = _mixer(y_sample, mod, p, l, batch=dec_batch, seq_len=dec_seq, n_seq=1, latent=True,
                       ctx=ctx, rope=rope)
        y_sample = _ffn(x1, mod, p, l, seq_len=dec_seq, latent=True, tm=FF_TILE)

    new_k, new_v, new_s = caches
    return (y_prompt.reshape(batch, seq, D_MODEL), y_sample.reshape(dec_batch, dec_seq, D_MODEL),
            new_k.reshape(batch, DEPTH, seq, N_KV_HEADS, HEAD_DIM),
            new_v.reshape(batch, DEPTH, seq, N_KV_HEADS, HEAD_DIM), new_s)
```

```python
import functools
import math

import jax
import jax.numpy as jnp
from jax import lax
from jax.experimental import pallas as pl
from jax.experimental.pallas import tpu as pltpu

D_MODEL = 1024
DEPTH = 2
GRID_W = 64
CHUNK = 128
HEAD_DIM = 64
ATTN_W = 512
N_HEADS = 8
N_KV_HEADS = 2
KV_W = 128
LRU_W = 256
LRU_BLOCKS = 4
LRU_BW = 64
CONV_W = 4
RG_C = 8.0
MLP_W = 256
MLP_GROUPS = 4
MLP_GW = 64
MIX_W = 1024
IN_W = 1792
D_FF = 4096
ROPE_THETA = 10000.0
ALPHA = (2 * DEPTH) ** 0.25
EPS = 1e-6

Q0, K0, V0, XR0, GR0, ZM0 = 0, 512, 640, 768, 1024, 1280
QK_W = ATTN_W + KV_W
LANES = 128
SUBLANES = 8
N_QGROUPS = ATTN_W // LANES
HEADS_PER_KV = N_HEADS // N_KV_HEADS
Q_BLOCK = 256
FF_CHUNK = 1024
FF_TILE = 1024
FF_ROWS = 512
VMEM_LIMIT = 60 * 1024 * 1024
Q_SCALE = HEAD_DIM ** -0.5 * math.log2(math.e)

F32 = jnp.float32
BF16 = jnp.bfloat16


def _layernorm(x, g, b):
    mu = jnp.mean(x, -1, keepdims=True)
    xc = x - mu
    var = jnp.mean(xc * xc, -1, keepdims=True)
    return xc * lax.rsqrt(var + EPS) * g + b


def _gelu(x):
    return jax.nn.gelu(x, approximate=True)


def _layer_spec(arr, layer):
    n = arr.ndim - 1
    return pl.BlockSpec((None,) + arr.shape[1:], lambda *_: (layer,) + (0,) * n, pipeline_mode=pl.Buffered(1))


def _const_spec(arr):
    n = arr.ndim
    return pl.BlockSpec(arr.shape, lambda *_: (0,) * n, pipeline_mode=pl.Buffered(1))


def _mod_kernel(cond_ref, w_ref, b_ref, o_ref):
    cond = cond_ref[...]
    o_ref[...] = jnp.dot(jax.nn.silu(cond).astype(BF16), w_ref[...].astype(BF16),
                         preferred_element_type=F32) + b_ref[...]


def _modulation(cond, w_ada, b_ada):
    rows = cond.shape[0]
    tn = 1536
    return pl.pallas_call(
        _mod_kernel,
        grid=(DEPTH, 6 * D_MODEL // tn),
        in_specs=[pl.BlockSpec((rows, D_MODEL), lambda l, j: (0, 0)),
                  pl.BlockSpec((None, D_MODEL, tn), lambda l, j: (l, 0, j)),
                  pl.BlockSpec((None, 1, tn), lambda l, j: (l, 0, j))],
        out_specs=pl.BlockSpec((None, rows, tn), lambda l, j: (l, 0, j)),
        out_shape=jax.ShapeDtypeStruct((DEPTH, rows, 6 * D_MODEL), F32),
        compiler_params=pltpu.CompilerParams(dimension_semantics=("arbitrary", "arbitrary"),
                                             vmem_limit_bytes=VMEM_LIMIT),
        name="adaln_modulation",
    )(cond, w_ada, b_ada.reshape(DEPTH, 1, 6 * D_MODEL))


def _tile_scan(a, u, reverse):
    n_t, w = a.shape[0] // SUBLANES, a.shape[1]
    a3 = a.reshape(n_t, SUBLANES, w)
    u3 = u.reshape(n_t, SUBLANES, w)
    sub = lax.broadcasted_iota(jnp.int32, (1, SUBLANES, w), 1)
    d = 1
    while d < SUBLANES:
        keep = (sub < SUBLANES - d) if reverse else (sub >= d)
        shift = SUBLANES - d if reverse else d
        a_sh = pltpu.roll(a3, shift, 1)
        u_sh = pltpu.roll(u3, shift, 1)
        u3 = a3 * jnp.where(keep, u_sh, 0.0) + u3
        a3 = a3 * jnp.where(keep, a_sh, 1.0)
        d *= 2
    return a3.reshape(n_t * SUBLANES, w), u3.reshape(n_t * SUBLANES, w)


def _mixer_kernel(*refs, seq_len, n_seq, latent):
    it = iter(refs)
    x_ref, mod_ref, w_in_ref, hsum_ref, gqk_ref = next(it), next(it), next(it), next(it), next(it)
    if latent:
        cos_ref, sin_a_ref, sin_b_ref = next(it), next(it), next(it)
        ck_ref, cv_ref, st_ref = next(it), next(it), next(it)
    conv_w_ref, conv_b_ref, wg_ref, bg_ref, lam_ref = next(it), next(it), next(it), next(it), next(it)
    mlp_g_ref, mlp_b_ref, ws_ref, bsb_ref = next(it), next(it), next(it), next(it)
    w_out_ref, ln_g_ref, ln_b_ref = next(it), next(it), next(it)
    if not latent:
        next(it), next(it), next(it)
    x1_ref = next(it)
    if not latent:
        k_out_ref, v_out_ref, s_out_ref = next(it), next(it), next(it)
    z_s, q_s, kt_s, vp_s, o_s, mix_s = next(it), next(it), next(it), next(it), next(it), next(it)
    e_bufs, il_bufs = (next(it), next(it)), (next(it), next(it))
    af_s, uf_s, ab_s, ub_s = next(it), next(it), next(it), next(it)

    L = seq_len
    past = ck_ref.shape[0] if latent else 0
    T = past + L
    n_tiles = L // SUBLANES
    kv_w = HEADS_PER_KV * HEAD_DIM

    mod = mod_ref[...]
    sh1, sc1, g1 = mod[:, 0:D_MODEL], mod[:, D_MODEL:2 * D_MODEL], mod[:, 2 * D_MODEL:3 * D_MODEL]
    h = (x_ref[...] * (1.0 + sc1) + sh1).astype(BF16)
    zqkv = jnp.dot(h, w_in_ref[:, :XR0], preferred_element_type=F32)
    z_s[...] = jnp.dot(h, w_in_ref[:, XR0:], preferred_element_type=F32)

    qk = zqkv[:, Q0:Q0 + QK_W]
    ssq = jnp.dot((qk * qk).astype(BF16), hsum_ref[...], preferred_element_type=F32)
    qkn = qk * lax.rsqrt(ssq * (1.0 / HEAD_DIM) + EPS) * gqk_ref[...]

    lam = lam_ref[...]
    neg = -lam
    decay = -RG_C * (jnp.maximum(neg, 0.0) + jnp.log1p(jnp.exp(-jnp.abs(neg))))
    quarter = lax.broadcasted_iota(jnp.int32, (1, kv_w), 1) // HEAD_DIM

    for s in range(n_seq):
        r0 = s * L
        rows = slice(r0, r0 + L)
        qkn_s = qkn[rows]
        k_cur = qkn_s[:, ATTN_W:]
        v_cur = zqkv[rows, V0:V0 + KV_W]
        if latent:
            cos, sin_a, sin_b = cos_ref[...], sin_a_ref[...], sin_b_ref[...]

            def rope(t):
                return (t * cos + pltpu.roll(t, LANES - 1, 1) * sin_a + pltpu.roll(t, 1, 1) * sin_b)

            for g in range(N_QGROUPS):
                qg = rope(qkn_s[:, g * LANES:(g + 1) * LANES]) * Q_SCALE
                q_s[g, rows, :] = qg.astype(BF16)
            k_all = jnp.concatenate([ck_ref[...], rope(k_cur)], axis=0)
            v_all = jnp.concatenate([cv_ref[...], v_cur], axis=0)
        else:
            k_out_ref[s] = k_cur
            v_out_ref[s] = v_cur
            for g in range(N_QGROUPS):
                q_s[g, rows, :] = (qkn_s[:, g * LANES:(g + 1) * LANES] * Q_SCALE).astype(BF16)
            k_all, v_all = k_cur, v_cur

        kt = k_all.T
        zer = jnp.zeros((HEAD_DIM, T), F32)
        for j in range(N_KV_HEADS):
            kj = kt[j * HEAD_DIM:(j + 1) * HEAD_DIM]
            kt_s[s, 2 * j] = jnp.concatenate([kj, zer], axis=0).astype(BF16)
            kt_s[s, 2 * j + 1] = jnp.concatenate([zer, kj], axis=0).astype(BF16)
        v_rot = pltpu.roll(v_all, HEAD_DIM, 1)
        v_two = jnp.concatenate([v_all, v_all], axis=1)
        v_rot_two = jnp.concatenate([v_rot, v_rot], axis=1)
        for j in range(N_KV_HEADS):
            for hq in range(HEADS_PER_KV):
                src = v_two if hq % 2 == j else v_rot_two
                vp_s[s, j, hq] = jnp.where(quarter == hq, src, 0.0).astype(BF16)

    qb = min(Q_BLOCK, L)
    n_qb = L // qb
    n_units = n_seq * N_KV_HEADS * n_qb

    def unit_index(n):
        s = n // (N_KV_HEADS * n_qb)
        j = (n // n_qb) % N_KV_HEADS
        qrows = pl.ds(pl.multiple_of(s * L + (n % n_qb) * qb, qb), qb)
        return s, j, qrows

    def score_stage(n, e_ref, il_ref):
        s, j, qrows = unit_index(n)
        inv = None
        for hq in range(HEADS_PER_KV):
            sc = jnp.dot(q_s[2 * j + hq // 2, qrows, :], kt_s[s, 2 * j + hq % 2],
                         preferred_element_type=F32)
            e = jnp.exp2(sc - jnp.max(sc, -1, keepdims=True))
            e_ref[hq] = e.astype(BF16)
            il = 1.0 / jnp.sum(e, -1, keepdims=True)
            inv = il if inv is None else jnp.where(quarter >= hq, il, inv)
        il_ref[...] = inv

    def value_stage(n, e_ref, il_ref):
        s, j, qrows = unit_index(n)
        acc = None
        for hq in range(HEADS_PER_KV):
            o = jnp.dot(e_ref[hq], vp_s[s, j, hq], preferred_element_type=F32)
            acc = o if acc is None else acc + o
        o_s[j, qrows, :] = (acc * il_ref[...]).astype(BF16)

    def unit_pair(k, carry):
        score_stage(2 * k + 1, e_bufs[1], il_bufs[1])
        value_stage(2 * k, e_bufs[0], il_bufs[0])
        score_stage(2 * k + 2, e_bufs[0], il_bufs[0])
        value_stage(2 * k + 1, e_bufs[1], il_bufs[1])
        return carry

    score_stage(0, e_bufs[0], il_bufs[0])
    lax.fori_loop(0, n_units // 2 - 1, unit_pair, 0)
    score_stage(n_units - 1, e_bufs[1], il_bufs[1])
    value_stage(n_units - 2, e_bufs[0], il_bufs[0])
    value_stage(n_units - 1, e_bufs[1], il_bufs[1])

    for s in range(n_seq):
        r0 = s * L
        rows = slice(r0, r0 + L)
        xr = z_s[rows, 0:LRU_W]
        row = lax.broadcasted_iota(jnp.int32, (L, LRU_W), 0)
        cw = conv_w_ref[...]
        xc = (jnp.where(row >= 1, pltpu.roll(xr, 1, 0), 0.0) * cw[0:1]
              + xr * cw[1:2]
              + jnp.where(row < L - 1, pltpu.roll(xr, L - 1, 0), 0.0) * cw[2:3]
              + jnp.where(row < L - 2, pltpu.roll(xr, L - 2, 0), 0.0) * cw[3:4]
              + conv_b_ref[...])
        xcb = xc.astype(BF16)
        for d, (a_s, u_s) in enumerate(((af_s, uf_s), (ab_s, ub_s))):
            gates = jnp.dot(xcb, wg_ref[:, 2 * d * LRU_W:2 * (d + 1) * LRU_W],
                            preferred_element_type=F32) + bg_ref[:, 2 * d * LRU_W:2 * (d + 1) * LRU_W]
            r = jax.nn.sigmoid(gates[:, :LRU_W])
            i_g = jax.nn.sigmoid(gates[:, LRU_W:])
            log_a = decay[d:d + 1] * r
            a = jnp.exp(log_a)
            th = jnp.tanh(log_a)
            u = jnp.sqrt(-2.0 * th / (1.0 - th)) * i_g * xc
            a_t, u_t = _tile_scan(a, u, reverse=(d == 1))
            a_s[rows, :] = a_t
            u_s[rows, :] = u_t

        def tile_step(t, carry):
            cf, cb = carry
            rf = pl.ds(pl.multiple_of(r0 + t * SUBLANES, SUBLANES), SUBLANES)
            rb = pl.ds(pl.multiple_of(r0 + (n_tiles - 1 - t) * SUBLANES, SUBLANES), SUBLANES)
            hf = af_s[rf, :] * cf + uf_s[rf, :]
            hb = ab_s[rb, :] * cb + ub_s[rb, :]
            uf_s[rf, :] = hf
            ub_s[rb, :] = hb
            return (jnp.broadcast_to(hf[SUBLANES - 1:SUBLANES], (SUBLANES, LRU_W)),
                    jnp.broadcast_to(hb[0:1], (SUBLANES, LRU_W)))

        if latent:
            init = (jnp.broadcast_to(st_ref[0:1, :], (SUBLANES, LRU_W)),
                    jnp.broadcast_to(st_ref[1:2, :], (SUBLANES, LRU_W)))
        else:
            init = (jnp.zeros((SUBLANES, LRU_W), F32), jnp.zeros((SUBLANES, LRU_W), F32))
        fin_f, fin_b = lax.fori_loop(0, n_tiles, tile_step, init, unroll=4)
        if not latent:
            s_out_ref[s, 0:1, :] = fin_f[0:1]
            s_out_ref[s, 1:2, :] = fin_b[0:1]
        y = (uf_s[rows, :] + ub_s[rows, :]) * _gelu(z_s[rows, GR0 - XR0:GR0 - XR0 + LRU_W])
        mix_s[rows, 0:LRU_W] = y.astype(BF16)

        glane = lax.broadcasted_iota(jnp.int32, (CHUNK, MLP_W), 1) // MLP_GW
        for c in range(L // CHUNK):
            crow = slice(r0 + c * CHUNK, r0 + (c + 1) * CHUNK)
            zg = _gelu(z_s[crow, ZM0 - XR0:ZM0 - XR0 + 2 * MLP_W])
            u_m = zg[:, :MLP_W]
            vn = _layernorm(zg[:, MLP_W:], mlp_g_ref[...], mlp_b_ref[...]).astype(BF16)
            full = jnp.dot(ws_ref[...], vn, preferred_element_type=F32)
            sm = bsb_ref[...]
            for gi in range(MLP_GROUPS):
                sm = sm + jnp.where(glane == gi, full[gi * CHUNK:(gi + 1) * CHUNK], 0.0)
            mix_s[crow, LRU_W:] = (u_m * sm).astype(BF16)

    out = jnp.dot(mix_s[...], w_out_ref[ATTN_W:, :], preferred_element_type=F32)
    for j in range(N_KV_HEADS):
        out = out + jnp.dot(o_s[j], w_out_ref[j * kv_w:(j + 1) * kv_w, :], preferred_element_type=F32)
    x1_ref[...] = _layernorm(ALPHA * x_ref[...] + g1 * out, ln_g_ref[...], ln_b_ref[...])


def _mixer(x2, mod, p, layer, *, batch, seq_len, n_seq, latent, ctx=None, rope=None, caches=None):
    B, L = batch, seq_len
    M = n_seq * L
    n_steps = B // n_seq
    T = L + (ctx[0].shape[2] if latent else 0)

    mod_row = (lambda i: i) if latent else (lambda i: mod.shape[1] - 1)
    in_specs = [pl.BlockSpec((M, D_MODEL), lambda i: (i, 0)),
                pl.BlockSpec((None, None, 1, 6 * D_MODEL), lambda i: (layer, mod_row(i), 0, 0))]
    args = [x2, mod]
    for name in ("w_in", "hsum", "gqk"):
        a = p[name]
        in_specs.append(_const_spec(a) if name == "hsum" else _layer_spec(a, layer))
        args.append(a)
    if latent:
        for a in rope:
            in_specs.append(_const_spec(a))
            args.append(a)
        ck, cv, st = ctx
        past = ck.shape[2]
        in_specs += [pl.BlockSpec((None, None, past, KV_W), lambda i: (i, layer, 0, 0)),
                     pl.BlockSpec((None, None, past, KV_W), lambda i: (i, layer, 0, 0)),
                     pl.BlockSpec((None, None, 2, LRU_W), lambda i: (i, layer, 0, 0))]
        args += [ck, cv, st]
    for name in ("conv_w", "conv_b", "wg", "bg", "lam", "mlp_g", "mlp_b", "ws", "bsb", "w_out", "ln1_g", "ln1_b"):
        in_specs.append(_layer_spec(p[name], layer))
        args.append(p[name])

    out_shape = [jax.ShapeDtypeStruct((B * L, D_MODEL), F32)]
    out_specs = [pl.BlockSpec((M, D_MODEL), lambda i: (i, 0))]
    aliases = {}
    if not latent:
        if caches is None:
            caches = (lax.empty((B, DEPTH, L, KV_W), F32), lax.empty((B, DEPTH, L, KV_W), F32),
                      lax.empty((B, DEPTH, 2, LRU_W), F32))
        for c in caches:
            aliases[len(args)] = len(out_shape)
            in_specs.append(pl.BlockSpec(memory_space=pl.ANY))
            args.append(c)
            out_shape.append(jax.ShapeDtypeStruct(c.shape, F32))
        out_specs += [pl.BlockSpec((n_seq, None, L, KV_W), lambda i: (i, layer, 0, 0)),
                      pl.BlockSpec((n_seq, None, L, KV_W), lambda i: (i, layer, 0, 0)),
                      pl.BlockSpec((n_seq, None, 2, LRU_W), lambda i: (i, layer, 0, 0))]

    qb = min(Q_BLOCK, L)
    kv_w = HEADS_PER_KV * HEAD_DIM
    scratch = [pltpu.VMEM((M, IN_W - XR0), F32),
               pltpu.VMEM((N_QGROUPS, M, LANES), BF16),
               pltpu.VMEM((n_seq, 2 * N_KV_HEADS, LANES, T), BF16),
               pltpu.VMEM((n_seq, N_KV_HEADS, HEADS_PER_KV, T, kv_w), BF16),
               pltpu.VMEM((N_KV_HEADS, M, kv_w), BF16),
               pltpu.VMEM((M, LRU_W + MLP_W), BF16)]
    scratch += [pltpu.VMEM((HEADS_PER_KV, qb, T), BF16)] * 2
    scratch += [pltpu.VMEM((qb, kv_w), F32)] * 2
    scratch += [pltpu.VMEM((M, LRU_W), F32)] * 4

    return pl.pallas_call(
        functools.partial(_mixer_kernel, seq_len=L, n_seq=n_seq, latent=latent),
        grid=(n_steps,),
        in_specs=in_specs,
        out_specs=out_specs,
        out_shape=out_shape,
        scratch_shapes=scratch,
        input_output_aliases=aliases,
        compiler_params=pltpu.CompilerParams(dimension_semantics=("arbitrary",), vmem_limit_bytes=VMEM_LIMIT),
        name="mixer_latent" if latent else "mixer_context",
    )(*args)


def _ffn_kernel(xc_ref, xl_ref, mod_ref, w1_ref, b1_ref, w2_ref, b2_ref, g_ref, b_ref, yc_ref, yl_ref, *, n_ctx_tiles):
    mod = mod_ref[...]
    sh2, sc2, g2 = mod[:, 3 * D_MODEL:4 * D_MODEL], mod[:, 4 * D_MODEL:5 * D_MODEL], mod[:, 5 * D_MODEL:]

    def tile(x_ref, o_ref):
        for r in range(x_ref.shape[0] // FF_ROWS):
            rows = slice(r * FF_ROWS, (r + 1) * FF_ROWS)
            x = x_ref[rows, :]
            h = (x * (1.0 + sc2) + sh2).astype(BF16)
            acc = None
            for c in range(D_FF // FF_CHUNK):
                cols = slice(c * FF_CHUNK, (c + 1) * FF_CHUNK)
                t = jnp.dot(h, w1_ref[:, cols], preferred_element_type=F32) + b1_ref[:, cols]
                t = jnp.square(jnp.maximum(t, 0.0)).astype(BF16)
                o = jnp.dot(t, w2_ref[cols, :], preferred_element_type=F32)
                acc = o if acc is None else acc + o
            f = acc + b2_ref[...]
            o_ref[rows, :] = _layernorm(ALPHA * x + g2 * f, g_ref[...], b_ref[...])

    is_ctx = pl.program_id(0) < n_ctx_tiles
    pl.when(is_ctx)(lambda: tile(xc_ref, yc_ref))
    pl.when(jnp.logical_not(is_ctx))(lambda: tile(xl_ref, yl_ref))


def _ffn(x_ctx, x_lat, mod, p, layer, *, lat_seq_len, tm):
    nc, nl = x_ctx.shape[0] // tm, x_lat.shape[0] // tm
    ctx_row = mod.shape[1] - 1

    def mod_row(i):
        return jnp.where(i < nc, ctx_row, (jnp.maximum(i - nc, 0) * tm) // lat_seq_len)

    ctx_tile = pl.BlockSpec((tm, D_MODEL), lambda i: (jnp.minimum(i, nc - 1), 0))
    lat_tile = pl.BlockSpec((tm, D_MODEL), lambda i: (jnp.maximum(i - nc, 0), 0))
    in_specs = [ctx_tile, lat_tile,
                pl.BlockSpec((None, None, 1, 6 * D_MODEL), lambda i: (layer, mod_row(i), 0, 0))]
    args = [x_ctx, x_lat, mod]
    for name in ("w_ff1", "b_ff1", "w_ff2", "b_ff2", "ln2_g", "ln2_b"):
        in_specs.append(_layer_spec(p[name], layer))
        args.append(p[name])
    return pl.pallas_call(
        functools.partial(_ffn_kernel, n_ctx_tiles=nc),
        grid=(nc + nl,),
        in_specs=in_specs,
        out_specs=[ctx_tile, lat_tile],
        out_shape=[jax.ShapeDtypeStruct(x_ctx.shape, F32), jax.ShapeDtypeStruct(x_lat.shape, F32)],
        compiler_params=pltpu.CompilerParams(dimension_semantics=("arbitrary",), vmem_limit_bytes=VMEM_LIMIT),
        name="ffn",
    )(*args)


def _rope_tables(length):
    rows = length // GRID_W
    pos_row = jnp.repeat(jnp.arange(rows), GRID_W).astype(F32)
    pos_col = jnp.tile(jnp.arange(GRID_W), rows).astype(F32)
    n_f = HEAD_DIM // 4
    inv = ROPE_THETA ** (-jnp.arange(n_f, dtype=F32) / n_f)
    ang = jnp.concatenate([pos_row[:, None] * inv, pos_col[:, None] * inv], -1)
    cos = jnp.repeat(jnp.cos(ang), 2, axis=-1)
    sin = jnp.repeat(jnp.sin(ang), 2, axis=-1)
    even = (jnp.arange(HEAD_DIM) % 2 == 0)[None, :]
    sin_a = jnp.where(even, -sin, 0.0)
    sin_b = jnp.where(even, 0.0, sin)
    rep = LANES // HEAD_DIM
    return tuple(jnp.tile(t, (1, rep)).astype(F32) for t in (cos, sin_a, sin_b))


def _block_diag(w):
    eye = jnp.eye(LRU_BLOCKS, dtype=w.dtype)
    full = w[..., :, :, None, :] * eye[:, None, :, None]
    return full.reshape(w.shape[:-3] + (LRU_W, LRU_W))


def _row(a):
    return a.reshape(a.shape[0], 1, a.shape[1])


def kernel(x_prompt, x_sample, c, cache_k, cache_v, state_lru, c_ctx, w_ada, b_ada, w_in, q_norm_g, k_norm_g,
           conv_w, conv_b, lru_wa, lru_ba, lru_wx, lru_bx, lru_lam, mlp_norm_g, mlp_norm_b, mlp_ws, mlp_bs,
           w_out, ln1_g, ln1_b, w_ff1, b_ff1, w_ff2, b_ff2, ln2_g, ln2_b):
    batch, seq, _ = x_prompt.shape
    dec_batch, dec_seq, _ = x_sample.shape
    past = cache_k.shape[2]

    n_cond = dec_batch + 1
    cond_rows = -(-n_cond // SUBLANES) * SUBLANES
    cond = jnp.concatenate([c, c_ctx[None, :], jnp.zeros((cond_rows - n_cond, D_MODEL), F32)], axis=0)
    mod = _modulation(cond, w_ada, b_ada)[:, :n_cond].reshape(DEPTH, n_cond, 1, 6 * D_MODEL)

    head_id = jnp.arange(QK_W) // HEAD_DIM
    wg = jnp.stack([_block_diag(lru_wa), _block_diag(lru_wx)], axis=2)
    wg = wg.transpose(0, 3, 1, 2, 4).reshape(DEPTH, LRU_W, 4 * LRU_W)
    p = dict(
        w_in=w_in.astype(BF16),
        hsum=(head_id[:, None] == head_id[None, :]).astype(BF16),
        gqk=_row(jnp.concatenate([jnp.tile(q_norm_g, (1, N_HEADS)), jnp.tile(k_norm_g, (1, N_KV_HEADS))], axis=1)),
        conv_w=conv_w, conv_b=_row(conv_b),
        wg=wg.astype(BF16),
        bg=jnp.stack([lru_ba, lru_bx], axis=2).reshape(DEPTH, 1, 4 * LRU_W),
        lam=lru_lam,
        mlp_g=_row(mlp_norm_g), mlp_b=_row(mlp_norm_b),
        ws=mlp_ws.reshape(DEPTH, MLP_GROUPS * CHUNK, CHUNK).astype(BF16),
        bsb=jnp.repeat(mlp_bs.transpose(0, 2, 1), MLP_GW, axis=2),
        w_out=w_out.astype(BF16), ln1_g=_row(ln1_g), ln1_b=_row(ln1_b),
        w_ff1=w_ff1.astype(BF16), b_ff1=_row(b_ff1), w_ff2=w_ff2.astype(BF16), b_ff2=_row(b_ff2),
        ln2_g=_row(ln2_g), ln2_b=_row(ln2_b),
    )
    rope = _rope_tables(dec_seq)
    ctx = (cache_k.reshape(dec_batch, DEPTH, past, KV_W), cache_v.reshape(dec_batch, DEPTH, past, KV_W), state_lru)

    y_prompt = x_prompt.reshape(batch * seq, D_MODEL)
    y_sample = x_sample.reshape(dec_batch * dec_seq, D_MODEL)
    caches = None
    for l in range(DEPTH):
        x1_p, *caches = _mixer(y_prompt, mod, p, l, batch=batch, seq_len=seq, n_seq=2, latent=False, caches=caches)
        (x1_s,) = _mixer(y_sample, mod, p, l, batch=dec_batch, seq_len=dec_seq, n_seq=1, latent=True,
                         ctx=ctx, rope=rope)
        y_prompt, y_sample = _ffn(x1_p, x1_s, mod, p, l, lat_seq_len=dec_seq, tm=FF_TILE)

    new_k, new_v, new_s = caches
    return (y_prompt.reshape(batch, seq, D_MODEL), y_sample.reshape(dec_batch, dec_seq, D_MODEL),
            new_k.reshape(batch, DEPTH, seq, N_KV_HEADS, HEAD_DIM),
            new_v.reshape(batch, DEPTH, seq, N_KV_HEADS, HEAD_DIM), new_s)
```

```python
import functools
import math

import jax
import jax.numpy as jnp
from jax import lax
from jax.experimental import pallas as pl
from jax.experimental.pallas import tpu as pltpu

D_MODEL = 1024
DEPTH = 2
GRID_W = 64
CHUNK = 128
HEAD_DIM = 64
ATTN_W = 512
N_HEADS = 8
N_KV_HEADS = 2
KV_W = 128
LRU_W = 256
LRU_BLOCKS = 4
LRU_BW = 64
CONV_W = 4
RG_C = 8.0
MLP_W = 256
MLP_GROUPS = 4
MLP_GW = 64
MIX_W = 1024
IN_W = 1792
D_FF = 4096
ROPE_THETA = 10000.0
ALPHA = (2 * DEPTH) ** 0.25
EPS = 1e-6

Q0, K0, V0, XR0, GR0, ZM0 = 0, 512, 640, 768, 1024, 1280
QK_W = ATTN_W + KV_W
LANES = 128
SUBLANES = 8
N_QGROUPS = ATTN_W // LANES
HEADS_PER_KV = N_HEADS // N_KV_HEADS
Q_BLOCK = 256
FF_CHUNK = 1024
FF_TILE = 1024
FF_ROWS = 512
VMEM_LIMIT = 60 * 1024 * 1024
Q_SCALE = HEAD_DIM ** -0.5 * math.log2(math.e)

F32 = jnp.float32
BF16 = jnp.bfloat16


def _layernorm(x, g, b):
    mu = jnp.mean(x, -1, keepdims=True)
    xc = x - mu
    var = jnp.mean(xc * xc, -1, keepdims=True)
    return xc * lax.rsqrt(var + EPS) * g + b


def _gelu(x):
    k = -2.0 * math.sqrt(2.0 / math.pi) * math.log2(math.e)
    return x / (1.0 + jnp.exp2(x * (k + (k * 0.044715) * (x * x))))


def _layer_spec(arr, layer):
    n = arr.ndim - 1
    return pl.BlockSpec((None,) + arr.shape[1:], lambda *_: (layer,) + (0,) * n, pipeline_mode=pl.Buffered(1))


def _const_spec(arr):
    n = arr.ndim
    return pl.BlockSpec(arr.shape, lambda *_: (0,) * n, pipeline_mode=pl.Buffered(1))


def _mod_kernel(cond_ref, w_ref, b_ref, o_ref):
    cond = cond_ref[...]
    o_ref[...] = jnp.dot(jax.nn.silu(cond).astype(BF16), w_ref[...].astype(BF16),
                         preferred_element_type=F32) + b_ref[...]


def _modulation(cond, w_ada, b_ada):
    rows = cond.shape[0]
    tn = 1536
    return pl.pallas_call(
        _mod_kernel,
        grid=(DEPTH, 6 * D_MODEL // tn),
        in_specs=[pl.BlockSpec((rows, D_MODEL), lambda l, j: (0, 0)),
                  pl.BlockSpec((None, D_MODEL, tn), lambda l, j: (l, 0, j)),
                  pl.BlockSpec((None, 1, tn), lambda l, j: (l, 0, j))],
        out_specs=pl.BlockSpec((None, rows, tn), lambda l, j: (l, 0, j)),
        out_shape=jax.ShapeDtypeStruct((DEPTH, rows, 6 * D_MODEL), F32),
        compiler_params=pltpu.CompilerParams(dimension_semantics=("arbitrary", "arbitrary"),
                                             vmem_limit_bytes=VMEM_LIMIT),
        name="adaln_modulation",
    )(cond, w_ada, b_ada.reshape(DEPTH, 1, 6 * D_MODEL))


def _tile_scan(a, u, reverse):
    n_t, w = a.shape[0] // SUBLANES, a.shape[1]
    a3 = a.reshape(n_t, SUBLANES, w)
    u3 = u.reshape(n_t, SUBLANES, w)
    sub = lax.broadcasted_iota(jnp.int32, (1, SUBLANES, w), 1)
    d = 1
    while d < SUBLANES:
        keep = (sub < SUBLANES - d) if reverse else (sub >= d)
        shift = SUBLANES - d if reverse else d
        a_sh = pltpu.roll(a3, shift, 1)
        u_sh = pltpu.roll(u3, shift, 1)
        u3 = a3 * jnp.where(keep, u_sh, 0.0) + u3
        a3 = a3 * jnp.where(keep, a_sh, 1.0)
        d *= 2
    return a3.reshape(n_t * SUBLANES, w), u3.reshape(n_t * SUBLANES, w)


def _mixer_kernel(*refs, seq_len, n_seq, latent):
    it = iter(refs)
    x_ref, mod_ref, w_in_ref, hsum_ref, gqk_ref = next(it), next(it), next(it), next(it), next(it)
    if latent:
        cos_ref, sin_a_ref, sin_b_ref = next(it), next(it), next(it)
        ck_ref, cv_ref, st_ref = next(it), next(it), next(it)
    conv_w_ref, conv_b_ref, wg_ref, bg_ref, lam_ref = next(it), next(it), next(it), next(it), next(it)
    mlp_g_ref, mlp_b_ref, ws_ref, bsb_ref = next(it), next(it), next(it), next(it)
    w_out_ref, ln_g_ref, ln_b_ref = next(it), next(it), next(it)
    if not latent:
        next(it), next(it), next(it)
    x1_ref = next(it)
    if not latent:
        k_out_ref, v_out_ref, s_out_ref = next(it), next(it), next(it)
    z_s, q_s, kt_s, vp_s, o_s, mix_s = next(it), next(it), next(it), next(it), next(it), next(it)
    e_bufs, il_bufs = (next(it), next(it)), (next(it), next(it))
    af_s, uf_s, ab_s, ub_s = next(it), next(it), next(it), next(it)

    L = seq_len
    past = ck_ref.shape[0] if latent else 0
    T = past + L
    n_tiles = L // SUBLANES
    kv_w = HEADS_PER_KV * HEAD_DIM

    mod = mod_ref[...]
    sh1, sc1, g1 = mod[:, 0:D_MODEL], mod[:, D_MODEL:2 * D_MODEL], mod[:, 2 * D_MODEL:3 * D_MODEL]
    h = (x_ref[...] * (1.0 + sc1) + sh1).astype(BF16)
    z_s[...] = jnp.dot(h, w_in_ref[...], preferred_element_type=F32)

    qk = z_s[:, Q0:Q0 + QK_W]
    ssq = jnp.dot((qk * qk).astype(BF16), hsum_ref[...], preferred_element_type=F32)
    qkn = qk * lax.rsqrt(ssq * (1.0 / HEAD_DIM) + EPS) * gqk_ref[...]

    lam = lam_ref[...]
    neg = -lam
    decay = -RG_C * (jnp.maximum(neg, 0.0) + jnp.log1p(jnp.exp(-jnp.abs(neg))))
    quarter = lax.broadcasted_iota(jnp.int32, (1, kv_w), 1) // HEAD_DIM

    for s in range(n_seq):
        r0 = s * L
        rows = slice(r0, r0 + L)
        qkn_s = qkn[rows]
        k_cur = qkn_s[:, ATTN_W:]
        v_cur = z_s[rows, V0:V0 + KV_W]
        if latent:
            cos, sin_a, sin_b = cos_ref[...], sin_a_ref[...], sin_b_ref[...]

            def rope(t):
                return (t * cos + pltpu.roll(t, LANES - 1, 1) * sin_a + pltpu.roll(t, 1, 1) * sin_b)

            for g in range(N_QGROUPS):
                qg = rope(qkn_s[:, g * LANES:(g + 1) * LANES]) * Q_SCALE
                q_s[g, rows, :] = qg.astype(BF16)
            k_all = jnp.concatenate([ck_ref[...], rope(k_cur)], axis=0)
            v_all = jnp.concatenate([cv_ref[...], v_cur], axis=0)
        else:
            k_out_ref[s] = k_cur
            v_out_ref[s] = v_cur
            for g in range(N_QGROUPS):
                q_s[g, rows, :] = (qkn_s[:, g * LANES:(g + 1) * LANES] * Q_SCALE).astype(BF16)
            k_all, v_all = k_cur, v_cur

        kt = k_all.T
        zer = jnp.zeros((HEAD_DIM, T), F32)
        for j in range(N_KV_HEADS):
            kj = kt[j * HEAD_DIM:(j + 1) * HEAD_DIM]
            kt_s[s, 2 * j] = jnp.concatenate([kj, zer], axis=0).astype(BF16)
            kt_s[s, 2 * j + 1] = jnp.concatenate([zer, kj], axis=0).astype(BF16)
        v_rot = pltpu.roll(v_all, HEAD_DIM, 1)
        v_two = jnp.concatenate([v_all, v_all], axis=1)
        v_rot_two = jnp.concatenate([v_rot, v_rot], axis=1)
        for j in range(N_KV_HEADS):
            for hq in range(HEADS_PER_KV):
                src = v_two if hq % 2 == j else v_rot_two
                vp_s[s, j, hq] = jnp.where(quarter == hq, src, 0.0).astype(BF16)

    qb = min(Q_BLOCK, L)
    n_qb = L // qb
    n_units = n_seq * N_KV_HEADS * n_qb

    def unit_index(n):
        s = n // (N_KV_HEADS * n_qb)
        j = (n // n_qb) % N_KV_HEADS
        qrows = pl.ds(pl.multiple_of(s * L + (n % n_qb) * qb, qb), qb)
        return s, j, qrows

    def score_stage(n, e_ref, il_ref):
        s, j, qrows = unit_index(n)
        inv = None
        for hq in range(HEADS_PER_KV):
            sc = jnp.dot(q_s[2 * j + hq // 2, qrows, :], kt_s[s, 2 * j + hq % 2],
                         preferred_element_type=F32)
            e = jnp.exp2(sc - jnp.max(sc, -1, keepdims=True))
            e_ref[hq] = e.astype(BF16)
            il = 1.0 / jnp.sum(e, -1, keepdims=True)
            inv = il if inv is None else jnp.where(quarter >= hq, il, inv)
        il_ref[...] = inv

    def value_stage(n, e_ref, il_ref):
        s, j, qrows = unit_index(n)
        acc = None
        for hq in range(HEADS_PER_KV):
            o = jnp.dot(e_ref[hq], vp_s[s, j, hq], preferred_element_type=F32)
            acc = o if acc is None else acc + o
        o_s[j, qrows, :] = (acc * il_ref[...]).astype(BF16)

    def unit_pair(k, carry):
        score_stage(2 * k + 1, e_bufs[1], il_bufs[1])
        value_stage(2 * k, e_bufs[0], il_bufs[0])
        score_stage(2 * k + 2, e_bufs[0], il_bufs[0])
        value_stage(2 * k + 1, e_bufs[1], il_bufs[1])
        return carry

    score_stage(0, e_bufs[0], il_bufs[0])
    lax.fori_loop(0, n_units // 2 - 1, unit_pair, 0)
    score_stage(n_units - 1, e_bufs[1], il_bufs[1])
    value_stage(n_units - 2, e_bufs[0], il_bufs[0])
    value_stage(n_units - 1, e_bufs[1], il_bufs[1])

    for s in range(n_seq):
        r0 = s * L
        rows = slice(r0, r0 + L)
        xr = z_s[rows, XR0:XR0 + LRU_W]
        row = lax.broadcasted_iota(jnp.int32, (L, LRU_W), 0)
        cw = conv_w_ref[...]
        xc = (jnp.where(row >= 1, pltpu.roll(xr, 1, 0), 0.0) * cw[0:1]
              + xr * cw[1:2]
              + jnp.where(row < L - 1, pltpu.roll(xr, L - 1, 0), 0.0) * cw[2:3]
              + jnp.where(row < L - 2, pltpu.roll(xr, L - 2, 0), 0.0) * cw[3:4]
              + conv_b_ref[...])
        xcb = xc.astype(BF16)
        xc2 = xc * math.sqrt(2.0)
        for d, (a_s, u_s) in enumerate(((af_s, uf_s), (ab_s, ub_s))):
            gates = jnp.dot(xcb, wg_ref[:, 2 * d * LRU_W:2 * (d + 1) * LRU_W],
                            preferred_element_type=F32) + bg_ref[:, 2 * d * LRU_W:2 * (d + 1) * LRU_W]
            r = jax.nn.sigmoid(gates[:, :LRU_W])
            i_g = jax.nn.sigmoid(gates[:, LRU_W:])
            log_a = decay[d:d + 1] * r
            a = jnp.exp(log_a)
            th = jnp.tanh(log_a)
            w = th / (th - 1.0)
            u = jnp.where(w > 0.0, w * lax.rsqrt(w), 0.0) * i_g * xc2
            a_t, u_t = _tile_scan(a, u, reverse=(d == 1))
            a_s[rows, :] = a_t
            u_s[rows, :] = u_t

        def tile_step(t, carry):
            cf, cb = carry
            rf = pl.ds(pl.multiple_of(r0 + t * SUBLANES, SUBLANES), SUBLANES)
            rb = pl.ds(pl.multiple_of(r0 + (n_tiles - 1 - t) * SUBLANES, SUBLANES), SUBLANES)
            hf = af_s[rf, :] * cf + uf_s[rf, :]
            hb = ab_s[rb, :] * cb + ub_s[rb, :]
            uf_s[rf, :] = hf
            ub_s[rb, :] = hb
            return (jnp.broadcast_to(hf[SUBLANES - 1:SUBLANES], (SUBLANES, LRU_W)),
                    jnp.broadcast_to(hb[0:1], (SUBLANES, LRU_W)))

        if latent:
            init = (jnp.broadcast_to(st_ref[0:1, :], (SUBLANES, LRU_W)),
                    jnp.broadcast_to(st_ref[1:2, :], (SUBLANES, LRU_W)))
        else:
            init = (jnp.zeros((SUBLANES, LRU_W), F32), jnp.zeros((SUBLANES, LRU_W), F32))
        fin_f, fin_b = lax.fori_loop(0, n_tiles, tile_step, init, unroll=4)
        if not latent:
            s_out_ref[s, 0:1, :] = fin_f[0:1]
            s_out_ref[s, 1:2, :] = fin_b[0:1]
        y = (uf_s[rows, :] + ub_s[rows, :]) * _gelu(z_s[rows, GR0:GR0 + LRU_W])
        mix_s[rows, 0:LRU_W] = y.astype(BF16)

        glane = lax.broadcasted_iota(jnp.int32, (CHUNK, MLP_W), 1) // MLP_GW
        for c in range(L // CHUNK):
            crow = slice(r0 + c * CHUNK, r0 + (c + 1) * CHUNK)
            zg = _gelu(z_s[crow, ZM0:ZM0 + 2 * MLP_W])
            u_m = zg[:, :MLP_W]
            vn = _layernorm(zg[:, MLP_W:], mlp_g_ref[...], mlp_b_ref[...]).astype(BF16)
            full = jnp.dot(ws_ref[...], vn, preferred_element_type=F32)
            sm = bsb_ref[...]
            for gi in range(MLP_GROUPS):
                sm = sm + jnp.where(glane == gi, full[gi * CHUNK:(gi + 1) * CHUNK], 0.0)
            mix_s[crow, LRU_W:] = (u_m * sm).astype(BF16)

    mixed = jnp.concatenate([o_s[j] for j in range(N_KV_HEADS)] + [mix_s[...]], axis=1)
    out = jnp.dot(mixed, w_out_ref[...], preferred_element_type=F32)
    x1_ref[...] = _layernorm(ALPHA * x_ref[...] + g1 * out, ln_g_ref[...], ln_b_ref[...])


def _mixer(x2, mod, p, layer, *, batch, seq_len, n_seq, latent, ctx=None, rope=None, caches=None):
    B, L = batch, seq_len
    M = n_seq * L
    n_steps = B // n_seq
    T = L + (ctx[0].shape[2] if latent else 0)

    mod_row = (lambda i: i) if latent else (lambda i: mod.shape[1] - 1)
    in_specs = [pl.BlockSpec((M, D_MODEL), lambda i: (i, 0)),
                pl.BlockSpec((None, None, 1, 6 * D_MODEL), lambda i: (layer, mod_row(i), 0, 0))]
    args = [x2, mod]
    for name in ("w_in", "hsum", "gqk"):
        a = p[name]
        in_specs.append(_const_spec(a) if name == "hsum" else _layer_spec(a, layer))
        args.append(a)
    if latent:
        for a in rope:
            in_specs.append(_const_spec(a))
            args.append(a)
        ck, cv, st = ctx
        past = ck.shape[2]
        in_specs += [pl.BlockSpec((None, None, past, KV_W), lambda i: (i, layer, 0, 0)),
                     pl.BlockSpec((None, None, past, KV_W), lambda i: (i, layer, 0, 0)),
                     pl.BlockSpec((None, None, 2, LRU_W), lambda i: (i, layer, 0, 0))]
        args += [ck, cv, st]
    for name in ("conv_w", "conv_b", "wg", "bg", "lam", "mlp_g", "mlp_b", "ws", "bsb", "w_out", "ln1_g", "ln1_b"):
        in_specs.append(_layer_spec(p[name], layer))
        args.append(p[name])

    out_shape = [jax.ShapeDtypeStruct((B * L, D_MODEL), F32)]
    out_specs = [pl.BlockSpec((M, D_MODEL), lambda i: (i, 0))]
    aliases = {}
    if not latent:
        if caches is None:
            caches = (jnp.zeros((B, DEPTH, L, KV_W), F32), jnp.zeros((B, DEPTH, L, KV_W), F32),
                      jnp.zeros((B, DEPTH, 2, LRU_W), F32))
        for c in caches:
            aliases[len(args)] = len(out_shape)
            in_specs.append(pl.BlockSpec(memory_space=pl.ANY))
            args.append(c)
            out_shape.append(jax.ShapeDtypeStruct(c.shape, F32))
        out_specs += [pl.BlockSpec((n_seq, None, L, KV_W), lambda i: (i, layer, 0, 0)),
                      pl.BlockSpec((n_seq, None, L, KV_W), lambda i: (i, layer, 0, 0)),
                      pl.BlockSpec((n_seq, None, 2, LRU_W), lambda i: (i, layer, 0, 0))]

    qb = min(Q_BLOCK, L)
    kv_w = HEADS_PER_KV * HEAD_DIM
    scratch = [pltpu.VMEM((M, IN_W), F32),
               pltpu.VMEM((N_QGROUPS, M, LANES), BF16),
               pltpu.VMEM((n_seq, 2 * N_KV_HEADS, LANES, T), BF16),
               pltpu.VMEM((n_seq, N_KV_HEADS, HEADS_PER_KV, T, kv_w), BF16),
               pltpu.VMEM((N_KV_HEADS, M, kv_w), BF16),
               pltpu.VMEM((M, LRU_W + MLP_W), BF16)]
    scratch += [pltpu.VMEM((HEADS_PER_KV, qb, T), BF16)] * 2
    scratch += [pltpu.VMEM((qb, kv_w), F32)] * 2
    scratch += [pltpu.VMEM((M, LRU_W), F32)] * 4

    return pl.pallas_call(
        functools.partial(_mixer_kernel, seq_len=L, n_seq=n_seq, latent=latent),
        grid=(n_steps,),
        in_specs=in_specs,
        out_specs=out_specs,
        out_shape=out_shape,
        scratch_shapes=scratch,
        input_output_aliases=aliases,
        compiler_params=pltpu.CompilerParams(dimension_semantics=("arbitrary",), vmem_limit_bytes=VMEM_LIMIT),
        name="mixer_latent" if latent else "mixer_context",
    )(*args)


def _ffn_kernel(x_ref, mod_ref, w1_ref, b1_ref, w2_ref, b2_ref, g_ref, b_ref, o_ref):
    mod = mod_ref[...]
    sh2, sc2, g2 = mod[:, 3 * D_MODEL:4 * D_MODEL], mod[:, 4 * D_MODEL:5 * D_MODEL], mod[:, 5 * D_MODEL:]
    for r in range(x_ref.shape[0] // FF_ROWS):
        rows = slice(r * FF_ROWS, (r + 1) * FF_ROWS)
        x = x_ref[rows, :]
        h = (x * (1.0 + sc2) + sh2).astype(BF16)
        acc = None
        for c in range(D_FF // FF_CHUNK):
            cols = slice(c * FF_CHUNK, (c + 1) * FF_CHUNK)
            t = jnp.dot(h, w1_ref[:, cols], preferred_element_type=F32) + b1_ref[:, cols]
            t = jnp.square(jnp.maximum(t, 0.0)).astype(BF16)
            o = jnp.dot(t, w2_ref[cols, :], preferred_element_type=F32)
            acc = o if acc is None else acc + o
        f = acc + b2_ref[...]
        o_ref[rows, :] = _layernorm(ALPHA * x + g2 * f, g_ref[...], b_ref[...])


def _ffn(x2, mod, p, layer, *, seq_len, latent, tm):
    rows = x2.shape[0]
    mod_row = (lambda i: (i * tm) // seq_len) if latent else (lambda i: mod.shape[1] - 1)
    in_specs = [pl.BlockSpec((tm, D_MODEL), lambda i: (i, 0)),
                pl.BlockSpec((None, None, 1, 6 * D_MODEL), lambda i: (layer, mod_row(i), 0, 0))]
    args = [x2, mod]
    for name in ("w_ff1", "b_ff1", "w_ff2", "b_ff2", "ln2_g", "ln2_b"):
        in_specs.append(_layer_spec(p[name], layer))
        args.append(p[name])
    return pl.pallas_call(
        _ffn_kernel,
        grid=(rows // tm,),
        in_specs=in_specs,
        out_specs=pl.BlockSpec((tm, D_MODEL), lambda i: (i, 0)),
        out_shape=jax.ShapeDtypeStruct((rows, D_MODEL), F32),
        compiler_params=pltpu.CompilerParams(dimension_semantics=("arbitrary",), vmem_limit_bytes=VMEM_LIMIT),
        name="ffn_latent" if latent else "ffn_context",
    )(*args)


def _rope_tables(length):
    rows = length // GRID_W
    pos_row = jnp.repeat(jnp.arange(rows), GRID_W).astype(F32)
    pos_col = jnp.tile(jnp.arange(GRID_W), rows).astype(F32)
    n_f = HEAD_DIM // 4
    inv = ROPE_THETA ** (-jnp.arange(n_f, dtype=F32) / n_f)
    ang = jnp.concatenate([pos_row[:, None] * inv, pos_col[:, None] * inv], -1)
    cos = jnp.repeat(jnp.cos(ang), 2, axis=-1)
    sin = jnp.repeat(jnp.sin(ang), 2, axis=-1)
    even = (jnp.arange(HEAD_DIM) % 2 == 0)[None, :]
    sin_a = jnp.where(even, -sin, 0.0)
    sin_b = jnp.where(even, 0.0, sin)
    rep = LANES // HEAD_DIM
    return tuple(jnp.tile(t, (1, rep)).astype(F32) for t in (cos, sin_a, sin_b))


def _block_diag(w):
    eye = jnp.eye(LRU_BLOCKS, dtype=w.dtype)
    full = w[..., :, :, None, :] * eye[:, None, :, None]
    return full.reshape(w.shape[:-3] + (LRU_W, LRU_W))


def _row(a):
    return a.reshape(a.shape[0], 1, a.shape[1])


def kernel(x_prompt, x_sample, c, cache_k, cache_v, state_lru, c_ctx, w_ada, b_ada, w_in, q_norm_g, k_norm_g,
           conv_w, conv_b, lru_wa, lru_ba, lru_wx, lru_bx, lru_lam, mlp_norm_g, mlp_norm_b, mlp_ws, mlp_bs,
           w_out, ln1_g, ln1_b, w_ff1, b_ff1, w_ff2, b_ff2, ln2_g, ln2_b):
    batch, seq, _ = x_prompt.shape
    dec_batch, dec_seq, _ = x_sample.shape
    past = cache_k.shape[2]

    n_cond = dec_batch + 1
    cond_rows = -(-n_cond // SUBLANES) * SUBLANES
    cond = jnp.concatenate([c, c_ctx[None, :], jnp.zeros((cond_rows - n_cond, D_MODEL), F32)], axis=0)
    mod = _modulation(cond, w_ada, b_ada)[:, :n_cond].reshape(DEPTH, n_cond, 1, 6 * D_MODEL)

    head_id = jnp.arange(QK_W) // HEAD_DIM
    wg = jnp.stack([_block_diag(lru_wa), _block_diag(lru_wx)], axis=2)
    wg = wg.transpose(0, 3, 1, 2, 4).reshape(DEPTH, LRU_W, 4 * LRU_W)
    p = dict(
        w_in=w_in.astype(BF16),
        hsum=(head_id[:, None] == head_id[None, :]).astype(BF16),
        gqk=_row(jnp.concatenate([jnp.tile(q_norm_g, (1, N_HEADS)), jnp.tile(k_norm_g, (1, N_KV_HEADS))], axis=1)),
        conv_w=conv_w, conv_b=_row(conv_b),
        wg=wg.astype(BF16),
        bg=jnp.stack([lru_ba, lru_bx], axis=2).reshape(DEPTH, 1, 4 * LRU_W),
        lam=lru_lam,
        mlp_g=_row(mlp_norm_g), mlp_b=_row(mlp_norm_b),
        ws=mlp_ws.reshape(DEPTH, MLP_GROUPS * CHUNK, CHUNK).astype(BF16),
        bsb=jnp.repeat(mlp_bs.transpose(0, 2, 1), MLP_GW, axis=2),
        w_out=w_out.astype(BF16), ln1_g=_row(ln1_g), ln1_b=_row(ln1_b),
        w_ff1=w_ff1.astype(BF16), b_ff1=_row(b_ff1), w_ff2=w_ff2.astype(BF16), b_ff2=_row(b_ff2),
        ln2_g=_row(ln2_g), ln2_b=_row(ln2_b),
    )
    rope = _rope_tables(dec_seq)
    ctx = (cache_k.reshape(dec_batch, DEPTH, past, KV_W), cache_v.reshape(dec_batch, DEPTH, past, KV_W), state_lru)

    y_prompt = x_prompt.reshape(batch * seq, D_MODEL)
    y_sample = x_sample.reshape(dec_batch * dec_seq, D_MODEL)
    caches = None
    for l in range(DEPTH):
        x1, *caches = _mixer(y_prompt, mod, p, l, batch=batch, seq_len=seq, n_seq=2, latent=False, caches=caches)
        y_prompt = _ffn(x1, mod, p, l, seq_len=seq, latent=False, tm=FF_TILE)
        (x1,) = _mixer(y_sample, mod, p, l, batch=dec_batch, seq_len=dec_seq, n_seq=1, latent=True,
                       ctx=ctx, rope=rope)
        y_sample = _ffn(x1, mod, p, l, seq_len=dec_seq, latent=True, tm=FF_TILE)

    new_k, new_v, new_s = caches
    return (y_prompt.reshape(batch, seq, D_MODEL), y_sample.reshape(dec_batch, dec_seq, D_MODEL),
            new_k.reshape(batch, DEPTH, seq, N_KV_HEADS, HEAD_DIM),
            new_v.reshape(batch, DEPTH, seq, N_KV_HEADS, HEAD_DIM), new_s)
```

```python
import functools
import math

import jax
import jax.numpy as jnp
import numpy as np
from jax import lax
from jax.experimental import pallas as pl
from jax.experimental.pallas import tpu as pltpu

D_MODEL = 1024
DEPTH = 2
GRID_W = 64
CHUNK = 128
HEAD_DIM = 64
ATTN_W = 512
N_HEADS = 8
N_KV_HEADS = 2
KV_W = 128
LRU_W = 256
LRU_BLOCKS = 4
LRU_BW = 64
CONV_W = 4
RG_C = 8.0
MLP_W = 256
MLP_GROUPS = 4
MLP_GW = 64
MIX_W = 1024
IN_W = 1792
D_FF = 4096
ROPE_THETA = 10000.0
ALPHA = (2 * DEPTH) ** 0.25
EPS = 1e-6

Q0, K0, V0, XR0, GR0, ZM0 = 0, 512, 640, 768, 1024, 1280
QK_W = ATTN_W + KV_W
LANES = 128
SUBLANES = 8
N_QGROUPS = ATTN_W // LANES
HEADS_PER_KV = N_HEADS // N_KV_HEADS
Q_BLOCK = 256
FF_CHUNK = 1024
FF_TILE = 1024
FF_ROWS = 512
VMEM_LIMIT = 60 * 1024 * 1024
Q_SCALE = HEAD_DIM ** -0.5 * math.log2(math.e)

F32 = jnp.float32
BF16 = jnp.bfloat16


def _layernorm(x, g, b):
    mu = jnp.mean(x, -1, keepdims=True)
    xc = x - mu
    var = jnp.mean(xc * xc, -1, keepdims=True)
    return xc * lax.rsqrt(var + EPS) * g + b


def _gelu(x):
    k = -2.0 * math.sqrt(2.0 / math.pi) * math.log2(math.e)
    return x / (1.0 + jnp.exp2(x * (k + (k * 0.044715) * (x * x))))


def _layer_spec(arr, layer):
    n = arr.ndim - 1
    return pl.BlockSpec((None,) + arr.shape[1:], lambda *_: (layer,) + (0,) * n, pipeline_mode=pl.Buffered(1))


def _const_spec(arr):
    n = arr.ndim
    return pl.BlockSpec(arr.shape, lambda *_: (0,) * n, pipeline_mode=pl.Buffered(1))


def _mod_kernel(cond_ref, w_ref, b_ref, o_ref):
    cond = cond_ref[...]
    o_ref[...] = jnp.dot(jax.nn.silu(cond).astype(BF16), w_ref[...].astype(BF16),
                         preferred_element_type=F32) + b_ref[...]


def _modulation(cond, w_ada, b_ada):
    rows = cond.shape[0]
    tn = 1536
    return pl.pallas_call(
        _mod_kernel,
        grid=(DEPTH, 6 * D_MODEL // tn),
        in_specs=[pl.BlockSpec((rows, D_MODEL), lambda l, j: (0, 0)),
                  pl.BlockSpec((None, D_MODEL, tn), lambda l, j: (l, 0, j)),
                  pl.BlockSpec((None, 1, tn), lambda l, j: (l, 0, j))],
        out_specs=pl.BlockSpec((None, rows, tn), lambda l, j: (l, 0, j)),
        out_shape=jax.ShapeDtypeStruct((DEPTH, rows, 6 * D_MODEL), F32),
        compiler_params=pltpu.CompilerParams(dimension_semantics=("arbitrary", "arbitrary"),
                                             vmem_limit_bytes=VMEM_LIMIT),
        name="adaln_modulation",
    )(cond, w_ada, b_ada.reshape(DEPTH, 1, 6 * D_MODEL))


def _tile_scan(a, u, reverse):
    n_t, w = a.shape[0] // SUBLANES, a.shape[1]
    a3 = a.reshape(n_t, SUBLANES, w)
    u3 = u.reshape(n_t, SUBLANES, w)
    sub = lax.broadcasted_iota(jnp.int32, (1, SUBLANES, w), 1)
    d = 1
    while d < SUBLANES:
        keep = (sub < SUBLANES - d) if reverse else (sub >= d)
        shift = SUBLANES - d if reverse else d
        a_sh = pltpu.roll(a3, shift, 1)
        u_sh = pltpu.roll(u3, shift, 1)
        u3 = a3 * jnp.where(keep, u_sh, 0.0) + u3
        a3 = a3 * jnp.where(keep, a_sh, 1.0)
        d *= 2
    return a3.reshape(n_t * SUBLANES, w), u3.reshape(n_t * SUBLANES, w)


def _mixer_kernel(*refs, seq_len, n_seq, latent, layer):
    it = iter(refs)
    lrow = slice(layer, layer + 1)
    x_ref, mod_ref, w_in_ref, hsum_ref, gqk_ref = next(it), next(it), next(it), next(it), next(it)
    if latent:
        cos_ref, sin_a_ref, sin_b_ref = next(it), next(it), next(it)
        ck_ref, cv_ref, st_ref = next(it), next(it), next(it)
    conv_w_ref, conv_b_ref, wg_ref, bg_ref, lam_ref = next(it), next(it), next(it), next(it), next(it)
    mlp_g_ref, mlp_b_ref, ws_ref, bsb_ref = next(it), next(it), next(it), next(it)
    w_out_ref, ln_g_ref, ln_b_ref = next(it), next(it), next(it)
    if not latent:
        next(it), next(it), next(it)
    x1_ref = next(it)
    if not latent:
        k_out_ref, v_out_ref, s_out_ref = next(it), next(it), next(it)
    z_s, q_s, kt_s, vp_s, o_s, mix_s = next(it), next(it), next(it), next(it), next(it), next(it)
    e_bufs, il_bufs = (next(it), next(it)), (next(it), next(it))
    af_s, uf_s, ab_s, ub_s = next(it), next(it), next(it), next(it)

    L = seq_len
    past = ck_ref.shape[0] if latent else 0
    T = past + L
    n_tiles = L // SUBLANES
    kv_w = HEADS_PER_KV * HEAD_DIM

    mod = mod_ref[...]
    sh1, sc1, g1 = mod[:, 0:D_MODEL], mod[:, D_MODEL:2 * D_MODEL], mod[:, 2 * D_MODEL:3 * D_MODEL]
    h = (x_ref[...] * (1.0 + sc1) + sh1).astype(BF16)
    z_s[...] = jnp.dot(h, w_in_ref[...], preferred_element_type=F32)

    qk = z_s[:, Q0:Q0 + QK_W]
    ssq = jnp.dot((qk * qk).astype(BF16), hsum_ref[...], preferred_element_type=F32)
    qkn = qk * lax.rsqrt(ssq * (1.0 / HEAD_DIM) + EPS) * gqk_ref[lrow, :]

    lam = lam_ref[...]
    neg = -lam
    decay = -RG_C * (jnp.maximum(neg, 0.0) + jnp.log1p(jnp.exp(-jnp.abs(neg))))
    quarter = lax.broadcasted_iota(jnp.int32, (1, kv_w), 1) // HEAD_DIM

    for s in range(n_seq):
        r0 = s * L
        rows = slice(r0, r0 + L)
        qkn_s = qkn[rows]
        k_cur = qkn_s[:, ATTN_W:]
        v_cur = z_s[rows, V0:V0 + KV_W]
        if latent:
            cos, sin_a, sin_b = cos_ref[...], sin_a_ref[...], sin_b_ref[...]

            def rope(t):
                return (t * cos + pltpu.roll(t, LANES - 1, 1) * sin_a + pltpu.roll(t, 1, 1) * sin_b)

            for g in range(N_QGROUPS):
                q_s[g, rows, :] = rope(qkn_s[:, g * LANES:(g + 1) * LANES]).astype(BF16)
            k_all = jnp.concatenate([ck_ref[...], rope(k_cur)], axis=0)
            v_all = jnp.concatenate([cv_ref[...], v_cur], axis=0)
        else:
            k_out_ref[s] = k_cur
            v_out_ref[s] = v_cur
            for g in range(N_QGROUPS):
                q_s[g, rows, :] = qkn_s[:, g * LANES:(g + 1) * LANES].astype(BF16)
            k_all, v_all = k_cur, v_cur

        kt = k_all.T.astype(BF16)
        zer = jnp.zeros((HEAD_DIM, T), BF16)
        for j in range(N_KV_HEADS):
            kj = kt[j * HEAD_DIM:(j + 1) * HEAD_DIM]
            kt_s[s, 2 * j] = jnp.concatenate([kj, zer], axis=0)
            kt_s[s, 2 * j + 1] = jnp.concatenate([zer, kj], axis=0)
        v_rot = pltpu.roll(v_all, HEAD_DIM, 1)
        v_two = jnp.concatenate([v_all, v_all], axis=1).astype(BF16)
        v_rot_two = jnp.concatenate([v_rot, v_rot], axis=1).astype(BF16)
        zero_v = jnp.zeros((), BF16)
        for j in range(N_KV_HEADS):
            for hq in range(HEADS_PER_KV):
                src = v_two if hq % 2 == j else v_rot_two
                vp_s[s, j, hq] = jnp.where(quarter == hq, src, zero_v)

    qb = min(Q_BLOCK, L)
    n_qb = L // qb
    n_units = n_seq * N_KV_HEADS * n_qb

    def unit_index(n):
        s = n // (N_KV_HEADS * n_qb)
        j = (n // n_qb) % N_KV_HEADS
        qrows = pl.ds(pl.multiple_of(s * L + (n % n_qb) * qb, qb), qb)
        return s, j, qrows

    def score_stage(n, e_ref, il_ref):
        s, j, qrows = unit_index(n)
        inv = None
        for hq in range(HEADS_PER_KV):
            sc = jnp.dot(q_s[2 * j + hq // 2, qrows, :], kt_s[s, 2 * j + hq % 2],
                         preferred_element_type=F32)
            e = jnp.exp2(sc - jnp.max(sc, -1, keepdims=True))
            e_ref[hq] = e.astype(BF16)
            il = 1.0 / jnp.sum(e, -1, keepdims=True)
            inv = il if inv is None else jnp.where(quarter >= hq, il, inv)
        il_ref[...] = inv

    def value_stage(n, e_ref, il_ref):
        s, j, qrows = unit_index(n)
        acc = None
        for hq in range(HEADS_PER_KV):
            o = jnp.dot(e_ref[hq], vp_s[s, j, hq], preferred_element_type=F32)
            acc = o if acc is None else acc + o
        o_s[j, qrows, :] = (acc * il_ref[...]).astype(BF16)

    def unit_pair(k, carry):
        score_stage(2 * k + 1, e_bufs[1], il_bufs[1])
        value_stage(2 * k, e_bufs[0], il_bufs[0])
        score_stage(2 * k + 2, e_bufs[0], il_bufs[0])
        value_stage(2 * k + 1, e_bufs[1], il_bufs[1])
        return carry

    score_stage(0, e_bufs[0], il_bufs[0])
    lax.fori_loop(0, n_units // 2 - 1, unit_pair, 0)
    score_stage(n_units - 1, e_bufs[1], il_bufs[1])
    value_stage(n_units - 2, e_bufs[0], il_bufs[0])
    value_stage(n_units - 1, e_bufs[1], il_bufs[1])

    for s in range(n_seq):
        r0 = s * L
        rows = slice(r0, r0 + L)
        xr = z_s[rows, XR0:XR0 + LRU_W]
        row = lax.broadcasted_iota(jnp.int32, (L, LRU_W), 0)
        cw = conv_w_ref[...]
        xc = (jnp.where(row >= 1, pltpu.roll(xr, 1, 0), 0.0) * cw[0:1]
              + xr * cw[1:2]
              + jnp.where(row < L - 1, pltpu.roll(xr, L - 1, 0), 0.0) * cw[2:3]
              + jnp.where(row < L - 2, pltpu.roll(xr, L - 2, 0), 0.0) * cw[3:4]
              + conv_b_ref[lrow, :])
        xcb = xc.astype(BF16)
        xc2 = xc * math.sqrt(2.0)
        for d, (a_s, u_s) in enumerate(((af_s, uf_s), (ab_s, ub_s))):
            gates = jnp.dot(xcb, wg_ref[:, 2 * d * LRU_W:2 * (d + 1) * LRU_W],
                            preferred_element_type=F32) + bg_ref[lrow, 2 * d * LRU_W:2 * (d + 1) * LRU_W]
            r = jax.nn.sigmoid(gates[:, :LRU_W])
            i_g = jax.nn.sigmoid(gates[:, LRU_W:])
            log_a = decay[d:d + 1] * r
            a = jnp.exp(log_a)
            th = jnp.tanh(log_a)
            w = th / (th - 1.0)
            u = jnp.where(w > 0.0, w * lax.rsqrt(w), 0.0) * i_g * xc2
            a_t, u_t = _tile_scan(a, u, reverse=(d == 1))
            a_s[rows, :] = a_t
            u_s[rows, :] = u_t

        def tile_step(t, carry):
            cf, cb = carry
            rf = pl.ds(pl.multiple_of(r0 + t * SUBLANES, SUBLANES), SUBLANES)
            rb = pl.ds(pl.multiple_of(r0 + (n_tiles - 1 - t) * SUBLANES, SUBLANES), SUBLANES)
            hf = af_s[rf, :] * cf + uf_s[rf, :]
            hb = ab_s[rb, :] * cb + ub_s[rb, :]
            uf_s[rf, :] = hf
            ub_s[rb, :] = hb
            return (jnp.broadcast_to(hf[SUBLANES - 1:SUBLANES], (SUBLANES, LRU_W)),
                    jnp.broadcast_to(hb[0:1], (SUBLANES, LRU_W)))

        if latent:
            init = (jnp.broadcast_to(st_ref[0:1, :], (SUBLANES, LRU_W)),
                    jnp.broadcast_to(st_ref[1:2, :], (SUBLANES, LRU_W)))
        else:
            init = (jnp.zeros((SUBLANES, LRU_W), F32), jnp.zeros((SUBLANES, LRU_W), F32))
        fin_f, fin_b = lax.fori_loop(0, n_tiles, tile_step, init, unroll=4)
        if not latent:
            s_out_ref[s, 0:1, :] = fin_f[0:1]
            s_out_ref[s, 1:2, :] = fin_b[0:1]
        y = (uf_s[rows, :] + ub_s[rows, :]) * _gelu(z_s[rows, GR0:GR0 + LRU_W])
        mix_s[rows, 0:LRU_W] = y.astype(BF16)

        glane = lax.broadcasted_iota(jnp.int32, (CHUNK, MLP_W), 1) // MLP_GW
        for c in range(L // CHUNK):
            crow = slice(r0 + c * CHUNK, r0 + (c + 1) * CHUNK)
            zg = _gelu(z_s[crow, ZM0:ZM0 + 2 * MLP_W])
            u_m = zg[:, :MLP_W]
            vn = _layernorm(zg[:, MLP_W:], mlp_g_ref[lrow, :], mlp_b_ref[lrow, :]).astype(BF16)
            full = jnp.dot(ws_ref[...], vn, preferred_element_type=F32)
            sm = full[(MLP_GROUPS - 1) * CHUNK:]
            for gi in range(MLP_GROUPS - 2, -1, -1):
                sm = jnp.where(glane == gi, full[gi * CHUNK:(gi + 1) * CHUNK], sm)
            mix_s[crow, LRU_W:] = (u_m * (sm + bsb_ref[...])).astype(BF16)

    mixed = jnp.concatenate([o_s[j] for j in range(N_KV_HEADS)] + [mix_s[...]], axis=1)
    out = jnp.dot(mixed, w_out_ref[...], preferred_element_type=F32)
    x1_ref[...] = _layernorm(ALPHA * x_ref[...] + g1 * out, ln_g_ref[lrow, :], ln_b_ref[lrow, :])


def _mixer(x2, mod, p, layer, *, batch, seq_len, n_seq, latent, ctx=None, rope=None, caches=None):
    B, L = batch, seq_len
    M = n_seq * L
    n_steps = B // n_seq
    T = L + (ctx[0].shape[2] if latent else 0)

    mod_row = (lambda i: i) if latent else (lambda i: mod.shape[1] - 1)
    in_specs = [pl.BlockSpec((M, D_MODEL), lambda i: (i, 0)),
                pl.BlockSpec((None, None, 1, 6 * D_MODEL), lambda i: (layer, mod_row(i), 0, 0))]
    args = [x2, mod]
    for name in ("w_in", "hsum", "gqk"):
        a = p[name]
        in_specs.append(_layer_spec(a, layer) if name == "w_in" else _const_spec(a))
        args.append(a)
    if latent:
        for a in rope:
            in_specs.append(_const_spec(a))
            args.append(a)
        ck, cv, st = ctx
        past = ck.shape[2]
        in_specs += [pl.BlockSpec((None, None, past, KV_W), lambda i: (i, layer, 0, 0)),
                     pl.BlockSpec((None, None, past, KV_W), lambda i: (i, layer, 0, 0)),
                     pl.BlockSpec((None, None, 2, LRU_W), lambda i: (i, layer, 0, 0))]
        args += [ck, cv, st]
    for name in ("conv_w", "conv_b", "wg", "bg", "lam", "mlp_g", "mlp_b", "ws", "bsb", "w_out", "ln1_g", "ln1_b"):
        a = p[name]
        in_specs.append(_const_spec(a) if a.ndim == 2 else _layer_spec(a, layer))
        args.append(a)

    out_shape = [jax.ShapeDtypeStruct((B * L, D_MODEL), F32)]
    out_specs = [pl.BlockSpec((M, D_MODEL), lambda i: (i, 0))]
    aliases = {}
    if not latent:
        if caches is None:
            caches = (jnp.zeros((B, DEPTH, L, KV_W), F32), jnp.zeros((B, DEPTH, L, KV_W), F32),
                      jnp.zeros((B, DEPTH, 2, LRU_W), F32))
        for c in caches:
            aliases[len(args)] = len(out_shape)
            in_specs.append(pl.BlockSpec(memory_space=pl.ANY))
            args.append(c)
            out_shape.append(jax.ShapeDtypeStruct(c.shape, F32))
        out_specs += [pl.BlockSpec((n_seq, None, L, KV_W), lambda i: (i, layer, 0, 0)),
                      pl.BlockSpec((n_seq, None, L, KV_W), lambda i: (i, layer, 0, 0)),
                      pl.BlockSpec((n_seq, None, 2, LRU_W), lambda i: (i, layer, 0, 0))]

    qb = min(Q_BLOCK, L)
    kv_w = HEADS_PER_KV * HEAD_DIM
    scratch = [pltpu.VMEM((M, IN_W), F32),
               pltpu.VMEM((N_QGROUPS, M, LANES), BF16),
               pltpu.VMEM((n_seq, 2 * N_KV_HEADS, LANES, T), BF16),
               pltpu.VMEM((n_seq, N_KV_HEADS, HEADS_PER_KV, T, kv_w), BF16),
               pltpu.VMEM((N_KV_HEADS, M, kv_w), BF16),
               pltpu.VMEM((M, LRU_W + MLP_W), BF16)]
    scratch += [pltpu.VMEM((HEADS_PER_KV, qb, T), BF16)] * 2
    scratch += [pltpu.VMEM((qb, kv_w), F32)] * 2
    scratch += [pltpu.VMEM((M, LRU_W), F32)] * 4

    return pl.pallas_call(
        functools.partial(_mixer_kernel, seq_len=L, n_seq=n_seq, latent=latent, layer=layer),
        grid=(n_steps,),
        in_specs=in_specs,
        out_specs=out_specs,
        out_shape=out_shape,
        scratch_shapes=scratch,
        input_output_aliases=aliases,
        compiler_params=pltpu.CompilerParams(dimension_semantics=("arbitrary",), vmem_limit_bytes=VMEM_LIMIT),
        name="mixer_latent" if latent else "mixer_context",
    )(*args)


def _ffn_kernel(x_ref, mod_ref, w1_ref, b1_ref, w2_ref, b2_ref, g_ref, b_ref, o_ref, *, layer):
    lrow = slice(layer, layer + 1)
    mod = mod_ref[...]
    sh2, sc2, g2 = mod[:, 3 * D_MODEL:4 * D_MODEL], mod[:, 4 * D_MODEL:5 * D_MODEL], mod[:, 5 * D_MODEL:]
    for r in range(x_ref.shape[0] // FF_ROWS):
        rows = slice(r * FF_ROWS, (r + 1) * FF_ROWS)
        x = x_ref[rows, :]
        h = (x * (1.0 + sc2) + sh2).astype(BF16)
        acc = None
        for c in range(D_FF // FF_CHUNK):
            cols = slice(c * FF_CHUNK, (c + 1) * FF_CHUNK)
            t = jnp.dot(h, w1_ref[:, cols], preferred_element_type=F32) + b1_ref[lrow, cols]
            t = jnp.square(jnp.maximum(t, 0.0)).astype(BF16)
            o = jnp.dot(t, w2_ref[cols, :], preferred_element_type=F32)
            acc = o if acc is None else acc + o
        f = acc + b2_ref[lrow, :]
        o_ref[rows, :] = _layernorm(ALPHA * x + g2 * f, g_ref[lrow, :], b_ref[lrow, :])


def _ffn(x2, mod, p, layer, *, seq_len, latent, tm):
    rows = x2.shape[0]
    mod_row = (lambda i: (i * tm) // seq_len) if latent else (lambda i: mod.shape[1] - 1)
    in_specs = [pl.BlockSpec((tm, D_MODEL), lambda i: (i, 0)),
                pl.BlockSpec((None, None, 1, 6 * D_MODEL), lambda i: (layer, mod_row(i), 0, 0))]
    args = [x2, mod]
    for name in ("w_ff1", "b_ff1", "w_ff2", "b_ff2", "ln2_g", "ln2_b"):
        a = p[name]
        in_specs.append(_const_spec(a) if a.ndim == 2 else _layer_spec(a, layer))
        args.append(a)
    return pl.pallas_call(
        functools.partial(_ffn_kernel, layer=layer),
        grid=(rows // tm,),
        in_specs=in_specs,
        out_specs=pl.BlockSpec((tm, D_MODEL), lambda i: (i, 0)),
        out_shape=jax.ShapeDtypeStruct((rows, D_MODEL), F32),
        compiler_params=pltpu.CompilerParams(dimension_semantics=("arbitrary",), vmem_limit_bytes=VMEM_LIMIT),
        name="ffn_latent" if latent else "ffn_context",
    )(*args)


def _rope_tables(length):
    rows = length // GRID_W
    pos_row = np.repeat(np.arange(rows), GRID_W).astype(np.float32)
    pos_col = np.tile(np.arange(GRID_W), rows).astype(np.float32)
    n_f = HEAD_DIM // 4
    inv = (ROPE_THETA ** (-np.arange(n_f, dtype=np.float32) / n_f)).astype(np.float32)
    ang = np.concatenate([pos_row[:, None] * inv, pos_col[:, None] * inv], -1)
    cos = np.repeat(np.cos(ang), 2, axis=-1)
    sin = np.repeat(np.sin(ang), 2, axis=-1)
    even = (np.arange(HEAD_DIM) % 2 == 0)[None, :]
    sin_a = np.where(even, -sin, 0.0)
    sin_b = np.where(even, 0.0, sin)
    rep = LANES // HEAD_DIM
    return tuple(jnp.asarray(np.tile(t, (1, rep)), F32) for t in (cos, sin_a, sin_b))


def _block_diag(w):
    eye = jnp.eye(LRU_BLOCKS, dtype=w.dtype)
    full = w[..., :, :, None, :] * eye[:, None, :, None]
    return full.reshape(w.shape[:-3] + (LRU_W, LRU_W))


def kernel(x_prompt, x_sample, c, cache_k, cache_v, state_lru, c_ctx, w_ada, b_ada, w_in, q_norm_g, k_norm_g,
           conv_w, conv_b, lru_wa, lru_ba, lru_wx, lru_bx, lru_lam, mlp_norm_g, mlp_norm_b, mlp_ws, mlp_bs,
           w_out, ln1_g, ln1_b, w_ff1, b_ff1, w_ff2, b_ff2, ln2_g, ln2_b):
    batch, seq, _ = x_prompt.shape
    dec_batch, dec_seq, _ = x_sample.shape
    past = cache_k.shape[2]

    n_cond = dec_batch + 1
    cond_rows = -(-n_cond // SUBLANES) * SUBLANES
    cond = jnp.concatenate([c, c_ctx[None, :], jnp.zeros((cond_rows - n_cond, D_MODEL), F32)], axis=0)
    mod = _modulation(cond, w_ada, b_ada)[:, :n_cond].reshape(DEPTH, n_cond, 1, 6 * D_MODEL)

    head_id = jnp.arange(QK_W) // HEAD_DIM
    wg = jnp.stack([_block_diag(lru_wa), _block_diag(lru_wx)], axis=2)
    wg = wg.transpose(0, 3, 1, 2, 4).reshape(DEPTH, LRU_W, 4 * LRU_W)
    p = dict(
        w_in=w_in.astype(BF16),
        hsum=(head_id[:, None] == head_id[None, :]).astype(BF16),
        gqk=jnp.concatenate([jnp.tile(q_norm_g * Q_SCALE, (1, N_HEADS)), jnp.tile(k_norm_g, (1, N_KV_HEADS))], axis=1),
        conv_w=conv_w, conv_b=conv_b,
        wg=wg.astype(BF16),
        bg=jnp.stack([lru_ba, lru_bx], axis=2).reshape(DEPTH, 4 * LRU_W),
        lam=lru_lam,
        mlp_g=mlp_norm_g, mlp_b=mlp_norm_b,
        ws=mlp_ws.reshape(DEPTH, MLP_GROUPS * CHUNK, CHUNK).astype(BF16),
        bsb=jnp.repeat(mlp_bs.transpose(0, 2, 1), MLP_GW, axis=2),
        w_out=w_out.astype(BF16), ln1_g=ln1_g, ln1_b=ln1_b,
        w_ff1=w_ff1.astype(BF16), b_ff1=b_ff1, w_ff2=w_ff2.astype(BF16), b_ff2=b_ff2,
        ln2_g=ln2_g, ln2_b=ln2_b,
    )
    rope = _rope_tables(dec_seq)
    ctx = (cache_k.reshape(dec_batch, DEPTH, past, KV_W), cache_v.reshape(dec_batch, DEPTH, past, KV_W), state_lru)

    y_prompt = x_prompt.reshape(batch * seq, D_MODEL)
    y_sample = x_sample.reshape(dec_batch * dec_seq, D_MODEL)
    caches = None
    for l in range(DEPTH):
        x1, *caches = _mixer(y_prompt, mod, p, l, batch=batch, seq_len=seq, n_seq=2, latent=False, caches=caches)
        y_prompt = _ffn(x1, mod, p, l, seq_len=seq, latent=False, tm=FF_TILE)
        (x1,) = _mixer(y_sample, mod, p, l, batch=dec_batch, seq_len=dec_seq, n_seq=1, latent=True,
                       ctx=ctx, rope=rope)
        y_sample = _ffn(x1, mod, p, l, seq_len=dec_seq, latent=True, tm=FF_TILE)

    new_k, new_v, new_s = caches
    return (y_prompt.reshape(batch, seq, D_MODEL), y_sample.reshape(dec_batch, dec_seq, D_MODEL),
            new_k.reshape(batch, DEPTH, seq, N_KV_HEADS, HEAD_DIM),
            new_v.reshape(batch, DEPTH, seq, N_KV_HEADS, HEAD_DIM), new_s)
```

```python
import functools
import math

import jax
import jax.numpy as jnp
import numpy as np
from jax import lax
from jax.experimental import pallas as pl
from jax.experimental.pallas import tpu as pltpu

D_MODEL = 1024
DEPTH = 2
GRID_W = 64
CHUNK = 128
HEAD_DIM = 64
ATTN_W = 512
N_HEADS = 8
N_KV_HEADS = 2
KV_W = 128
LRU_W = 256
LRU_BLOCKS = 4
LRU_BW = 64
CONV_W = 4
RG_C = 8.0
MLP_W = 256
MLP_GROUPS = 4
MLP_GW = 64
MIX_W = 1024
IN_W = 1792
D_FF = 4096
ROPE_THETA = 10000.0
ALPHA = (2 * DEPTH) ** 0.25
EPS = 1e-6

Q0, K0, V0, XR0, GR0, ZM0 = 0, 512, 640, 768, 1024, 1280
QK_W = ATTN_W + KV_W
LANES = 128
SUBLANES = 8
N_QGROUPS = ATTN_W // LANES
HEADS_PER_KV = N_HEADS // N_KV_HEADS
Q_BLOCK = 256
FF_CHUNK = 1024
FF_TILE = 1024
FF_ROWS = 512
VMEM_LIMIT = 60 * 1024 * 1024
Q_SCALE = HEAD_DIM ** -0.5 * math.log2(math.e)

F32 = jnp.float32
BF16 = jnp.bfloat16


def _layernorm(x, g, b):
    mu = jnp.mean(x, -1, keepdims=True)
    xc = x - mu
    var = jnp.mean(xc * xc, -1, keepdims=True)
    return xc * lax.rsqrt(var + EPS) * g + b


def _gelu(x):
    k = -2.0 * math.sqrt(2.0 / math.pi) * math.log2(math.e)
    return x / (1.0 + jnp.exp2(x * (k + (k * 0.044715) * (x * x))))


def _layer_spec(arr, layer):
    n = arr.ndim - 1
    return pl.BlockSpec((None,) + arr.shape[1:], lambda *_: (layer,) + (0,) * n, pipeline_mode=pl.Buffered(1))


def _const_spec(arr):
    n = arr.ndim
    return pl.BlockSpec(arr.shape, lambda *_: (0,) * n, pipeline_mode=pl.Buffered(1))


def _mod_kernel(cond_ref, w_ref, b_ref, o_ref):
    cond = cond_ref[...]
    o_ref[...] = jnp.dot(jax.nn.silu(cond).astype(BF16), w_ref[...].astype(BF16),
                         preferred_element_type=F32) + b_ref[...]


def _modulation(cond, w_ada, b_ada):
    rows = cond.shape[0]
    tn = 1536
    return pl.pallas_call(
        _mod_kernel,
        grid=(DEPTH, 6 * D_MODEL // tn),
        in_specs=[pl.BlockSpec((rows, D_MODEL), lambda l, j: (0, 0)),
                  pl.BlockSpec((None, D_MODEL, tn), lambda l, j: (l, 0, j)),
                  pl.BlockSpec((None, 1, tn), lambda l, j: (l, 0, j))],
        out_specs=pl.BlockSpec((None, rows, tn), lambda l, j: (l, 0, j)),
        out_shape=jax.ShapeDtypeStruct((DEPTH, rows, 6 * D_MODEL), F32),
        compiler_params=pltpu.CompilerParams(dimension_semantics=("arbitrary", "arbitrary"),
                                             vmem_limit_bytes=VMEM_LIMIT),
        name="adaln_modulation",
    )(cond, w_ada, b_ada.reshape(DEPTH, 1, 6 * D_MODEL))


def _tile_scan(a, u, reverse):
    n_t, w = a.shape[0] // SUBLANES, a.shape[1]
    a3 = a.reshape(n_t, SUBLANES, w)
    u3 = u.reshape(n_t, SUBLANES, w)
    sub = lax.broadcasted_iota(jnp.int32, (1, SUBLANES, w), 1)
    d = 1
    while d < SUBLANES:
        keep = (sub < SUBLANES - d) if reverse else (sub >= d)
        shift = SUBLANES - d if reverse else d
        a_sh = pltpu.roll(a3, shift, 1)
        u_sh = pltpu.roll(u3, shift, 1)
        u3 = a3 * jnp.where(keep, u_sh, 0.0) + u3
        a3 = a3 * jnp.where(keep, a_sh, 1.0)
        d *= 2
    return a3.reshape(n_t * SUBLANES, w), u3.reshape(n_t * SUBLANES, w)


def _mixer_kernel(*refs, seq_len, n_seq, latent, layer):
    it = iter(refs)
    lrow = slice(layer, layer + 1)
    x_ref, mod_ref, w_in_ref, hsum_ref, gqk_ref = next(it), next(it), next(it), next(it), next(it)
    if latent:
        cos_ref, sin_a_ref, sin_b_ref = next(it), next(it), next(it)
        ck_ref, cv_ref, st_ref = next(it), next(it), next(it)
    conv_w_ref, conv_b_ref, wg_ref, bg_ref, lam_ref = next(it), next(it), next(it), next(it), next(it)
    mlp_g_ref, mlp_b_ref, ws_ref, bsb_ref = next(it), next(it), next(it), next(it)
    w_out_ref, ln_g_ref, ln_b_ref = next(it), next(it), next(it)
    if not latent:
        next(it), next(it), next(it)
    x1_ref = next(it)
    if not latent:
        k_out_ref, v_out_ref, s_out_ref = next(it), next(it), next(it)
    z_s, q_s, kt_s, vp_s, o_s, mix_s = next(it), next(it), next(it), next(it), next(it), next(it)
    e_bufs, il_bufs = (next(it), next(it)), (next(it), next(it))
    af_s, uf_s, ab_s, ub_s = next(it), next(it), next(it), next(it)

    L = seq_len
    past = ck_ref.shape[0] if latent else 0
    T = past + L
    n_tiles = L // SUBLANES
    kv_w = HEADS_PER_KV * HEAD_DIM

    mod = mod_ref[...]
    sh1, sc1, g1 = mod[:, 0:D_MODEL], mod[:, D_MODEL:2 * D_MODEL], mod[:, 2 * D_MODEL:3 * D_MODEL]
    h = (x_ref[...] * (1.0 + sc1) + sh1).astype(BF16)
    z_s[...] = jnp.dot(h, w_in_ref[...], preferred_element_type=F32)

    qk = z_s[:, Q0:Q0 + QK_W]
    ssq = jnp.dot((qk * qk).astype(BF16), hsum_ref[...], preferred_element_type=F32)
    qkn = qk * lax.rsqrt(ssq * (1.0 / HEAD_DIM) + EPS) * gqk_ref[lrow, :]

    lam = lam_ref[...]
    neg = -lam
    decay = -RG_C * (jnp.maximum(neg, 0.0) + jnp.log1p(jnp.exp(-jnp.abs(neg))))
    quarter = lax.broadcasted_iota(jnp.int32, (1, kv_w), 1) // HEAD_DIM

    for s in range(n_seq):
        r0 = s * L
        rows = slice(r0, r0 + L)
        qkn_s = qkn[rows]
        k_cur = qkn_s[:, ATTN_W:]
        v_cur = z_s[rows, V0:V0 + KV_W]
        if latent:
            cos, sin_a, sin_b = cos_ref[...], sin_a_ref[...], sin_b_ref[...]

            def rope(t):
                return (t * cos + pltpu.roll(t, LANES - 1, 1) * sin_a + pltpu.roll(t, 1, 1) * sin_b)

            for g in range(N_QGROUPS):
                q_s[g, rows, :] = rope(qkn_s[:, g * LANES:(g + 1) * LANES]).astype(BF16)
            k_all = jnp.concatenate([ck_ref[...], rope(k_cur)], axis=0)
            v_all = jnp.concatenate([cv_ref[...], v_cur], axis=0)
        else:
            k_out_ref[s] = k_cur
            v_out_ref[s] = v_cur
            for g in range(N_QGROUPS):
                q_s[g, rows, :] = qkn_s[:, g * LANES:(g + 1) * LANES].astype(BF16)
            k_all, v_all = k_cur, v_cur

        kt = k_all.T.astype(BF16)
        zer = jnp.zeros((HEAD_DIM, T), BF16)
        for j in range(N_KV_HEADS):
            kj = kt[j * HEAD_DIM:(j + 1) * HEAD_DIM]
            kt_s[s, 2 * j] = jnp.concatenate([kj, zer], axis=0)
            kt_s[s, 2 * j + 1] = jnp.concatenate([zer, kj], axis=0)
        v_rot = pltpu.roll(v_all, HEAD_DIM, 1)
        v_two = jnp.concatenate([v_all, v_all], axis=1).astype(BF16)
        v_rot_two = jnp.concatenate([v_rot, v_rot], axis=1).astype(BF16)
        zero_v = jnp.zeros((), BF16)
        for j in range(N_KV_HEADS):
            for hq in range(HEADS_PER_KV):
                src = v_two if hq % 2 == j else v_rot_two
                vp_s[s, j, hq] = jnp.where(quarter == hq, src, zero_v)

    qb = min(Q_BLOCK, L)
    n_qb = L // qb
    n_units = n_seq * N_KV_HEADS * n_qb

    def unit_index(n):
        s = n // (N_KV_HEADS * n_qb)
        j = (n // n_qb) % N_KV_HEADS
        qrows = pl.ds(pl.multiple_of(s * L + (n % n_qb) * qb, qb), qb)
        return s, j, qrows

    def score_stage(n, e_ref, il_ref):
        s, j, qrows = unit_index(n)
        inv = None
        for hq in range(HEADS_PER_KV):
            sc = jnp.dot(q_s[2 * j + hq // 2, qrows, :], kt_s[s, 2 * j + hq % 2],
                         preferred_element_type=F32)
            e = jnp.exp2(sc - jnp.max(sc, -1, keepdims=True))
            e_ref[hq] = e.astype(BF16)
            il = 1.0 / jnp.sum(e, -1, keepdims=True)
            inv = il if inv is None else jnp.where(quarter >= hq, il, inv)
        il_ref[...] = inv

    def value_stage(n, e_ref, il_ref):
        s, j, qrows = unit_index(n)
        acc = None
        for hq in range(HEADS_PER_KV):
            o = jnp.dot(e_ref[hq], vp_s[s, j, hq], preferred_element_type=F32)
            acc = o if acc is None else acc + o
        o_s[j, qrows, :] = (acc * il_ref[...]).astype(BF16)

    def unit_pair(k, carry):
        score_stage(2 * k + 1, e_bufs[1], il_bufs[1])
        value_stage(2 * k, e_bufs[0], il_bufs[0])
        score_stage(2 * k + 2, e_bufs[0], il_bufs[0])
        value_stage(2 * k + 1, e_bufs[1], il_bufs[1])
        return carry

    score_stage(0, e_bufs[0], il_bufs[0])
    lax.fori_loop(0, n_units // 2 - 1, unit_pair, 0)
    score_stage(n_units - 1, e_bufs[1], il_bufs[1])
    value_stage(n_units - 2, e_bufs[0], il_bufs[0])
    value_stage(n_units - 1, e_bufs[1], il_bufs[1])

    for s in range(n_seq):
        r0 = s * L
        rows = slice(r0, r0 + L)
        xr = z_s[rows, XR0:XR0 + LRU_W]
        row = lax.broadcasted_iota(jnp.int32, (L, LRU_W), 0)
        cw = conv_w_ref[...]
        xc = (jnp.where(row >= 1, pltpu.roll(xr, 1, 0), 0.0) * cw[0:1]
              + xr * cw[1:2]
              + jnp.where(row < L - 1, pltpu.roll(xr, L - 1, 0), 0.0) * cw[2:3]
              + jnp.where(row < L - 2, pltpu.roll(xr, L - 2, 0), 0.0) * cw[3:4]
              + conv_b_ref[lrow, :])
        xcb = xc.astype(BF16)
        xc2 = xc * math.sqrt(2.0)
        for d, (a_s, u_s) in enumerate(((af_s, uf_s), (ab_s, ub_s))):
            gates = jnp.dot(xcb, wg_ref[:, 2 * d * LRU_W:2 * (d + 1) * LRU_W],
                            preferred_element_type=F32) + bg_ref[lrow, 2 * d * LRU_W:2 * (d + 1) * LRU_W]
            r = jax.nn.sigmoid(gates[:, :LRU_W])
            i_g = jax.nn.sigmoid(gates[:, LRU_W:])
            log_a = decay[d:d + 1] * r
            a = jnp.exp(log_a)
            th = jnp.tanh(log_a)
            w = th / (th - 1.0)
            u = jnp.where(w > 0.0, w * lax.rsqrt(w), 0.0) * i_g * xc2
            a_t, u_t = _tile_scan(a, u, reverse=(d == 1))
            a_s[rows, :] = a_t
            u_s[rows, :] = u_t

        def tile_step(t, carry):
            cf, cb = carry
            rf = slice(r0 + t * SUBLANES, r0 + (t + 1) * SUBLANES)
            rb = slice(r0 + (n_tiles - 1 - t) * SUBLANES, r0 + (n_tiles - t) * SUBLANES)
            hf = af_s[rf, :] * cf + uf_s[rf, :]
            hb = ab_s[rb, :] * cb + ub_s[rb, :]
            uf_s[rf, :] = hf
            ub_s[rb, :] = hb
            return (jnp.broadcast_to(hf[SUBLANES - 1:SUBLANES], (SUBLANES, LRU_W)),
                    jnp.broadcast_to(hb[0:1], (SUBLANES, LRU_W)))

        if latent:
            carry = (jnp.broadcast_to(st_ref[0:1, :], (SUBLANES, LRU_W)),
                     jnp.broadcast_to(st_ref[1:2, :], (SUBLANES, LRU_W)))
        else:
            carry = (jnp.zeros((SUBLANES, LRU_W), F32), jnp.zeros((SUBLANES, LRU_W), F32))
        for t in range(n_tiles):
            carry = tile_step(t, carry)
        fin_f, fin_b = carry
        if not latent:
            s_out_ref[s, 0:1, :] = fin_f[0:1]
            s_out_ref[s, 1:2, :] = fin_b[0:1]
        y = (uf_s[rows, :] + ub_s[rows, :]) * _gelu(z_s[rows, GR0:GR0 + LRU_W])
        mix_s[rows, 0:LRU_W] = y.astype(BF16)

        glane = lax.broadcasted_iota(jnp.int32, (CHUNK, MLP_W), 1) // MLP_GW
        for c in range(L // CHUNK):
            crow = slice(r0 + c * CHUNK, r0 + (c + 1) * CHUNK)
            zg = _gelu(z_s[crow, ZM0:ZM0 + 2 * MLP_W])
            u_m = zg[:, :MLP_W]
            vn = _layernorm(zg[:, MLP_W:], mlp_g_ref[lrow, :], mlp_b_ref[lrow, :]).astype(BF16)
            full = jnp.dot(ws_ref[...], vn, preferred_element_type=F32)
            sm = full[(MLP_GROUPS - 1) * CHUNK:]
            for gi in range(MLP_GROUPS - 2, -1, -1):
                sm = jnp.where(glane == gi, full[gi * CHUNK:(gi + 1) * CHUNK], sm)
            mix_s[crow, LRU_W:] = (u_m * (sm + bsb_ref[...])).astype(BF16)

    mixed = jnp.concatenate([o_s[j] for j in range(N_KV_HEADS)] + [mix_s[...]], axis=1)
    out = jnp.dot(mixed, w_out_ref[...], preferred_element_type=F32)
    x1_ref[...] = _layernorm(ALPHA * x_ref[...] + g1 * out, ln_g_ref[lrow, :], ln_b_ref[lrow, :])


def _mixer(x2, mod, p, layer, *, batch, seq_len, n_seq, latent, ctx=None, rope=None, caches=None):
    B, L = batch, seq_len
    M = n_seq * L
    n_steps = B // n_seq
    T = L + (ctx[0].shape[2] if latent else 0)

    mod_row = (lambda i: i) if latent else (lambda i: mod.shape[1] - 1)
    in_specs = [pl.BlockSpec((M, D_MODEL), lambda i: (i, 0)),
                pl.BlockSpec((None, None, 1, 6 * D_MODEL), lambda i: (layer, mod_row(i), 0, 0))]
    args = [x2, mod]
    for name in ("w_in", "hsum", "gqk"):
        a = p[name]
        in_specs.append(_layer_spec(a, layer) if name == "w_in" else _const_spec(a))
        args.append(a)
    if latent:
        for a in rope:
            in_specs.append(_const_spec(a))
            args.append(a)
        ck, cv, st = ctx
        past = ck.shape[2]
        in_specs += [pl.BlockSpec((None, None, past, KV_W), lambda i: (i, layer, 0, 0)),
                     pl.BlockSpec((None, None, past, KV_W), lambda i: (i, layer, 0, 0)),
                     pl.BlockSpec((None, None, 2, LRU_W), lambda i: (i, layer, 0, 0))]
        args += [ck, cv, st]
    for name in ("conv_w", "conv_b", "wg", "bg", "lam", "mlp_g", "mlp_b", "ws", "bsb", "w_out", "ln1_g", "ln1_b"):
        a = p[name]
        in_specs.append(_const_spec(a) if a.ndim == 2 else _layer_spec(a, layer))
        args.append(a)

    out_shape = [jax.ShapeDtypeStruct((B * L, D_MODEL), F32)]
    out_specs = [pl.BlockSpec((M, D_MODEL), lambda i: (i, 0))]
    aliases = {}
    if not latent:
        if caches is None:
            caches = (jnp.zeros((B, DEPTH, L, KV_W), F32), jnp.zeros((B, DEPTH, L, KV_W), F32),
                      jnp.zeros((B, DEPTH, 2, LRU_W), F32))
        for c in caches:
            aliases[len(args)] = len(out_shape)
            in_specs.append(pl.BlockSpec(memory_space=pl.ANY))
            args.append(c)
            out_shape.append(jax.ShapeDtypeStruct(c.shape, F32))
        out_specs += [pl.BlockSpec((n_seq, None, L, KV_W), lambda i: (i, layer, 0, 0)),
                      pl.BlockSpec((n_seq, None, L, KV_W), lambda i: (i, layer, 0, 0)),
                      pl.BlockSpec((n_seq, None, 2, LRU_W), lambda i: (i, layer, 0, 0))]

    qb = min(Q_BLOCK, L)
    kv_w = HEADS_PER_KV * HEAD_DIM
    scratch = [pltpu.VMEM((M, IN_W), F32),
               pltpu.VMEM((N_QGROUPS, M, LANES), BF16),
               pltpu.VMEM((n_seq, 2 * N_KV_HEADS, LANES, T), BF16),
               pltpu.VMEM((n_seq, N_KV_HEADS, HEADS_PER_KV, T, kv_w), BF16),
               pltpu.VMEM((N_KV_HEADS, M, kv_w), BF16),
               pltpu.VMEM((M, LRU_W + MLP_W), BF16)]
    scratch += [pltpu.VMEM((HEADS_PER_KV, qb, T), BF16)] * 2
    scratch += [pltpu.VMEM((qb, kv_w), F32)] * 2
    scratch += [pltpu.VMEM((M, LRU_W), F32)] * 4

    return pl.pallas_call(
        functools.partial(_mixer_kernel, seq_len=L, n_seq=n_seq, latent=latent, layer=layer),
        grid=(n_steps,),
        in_specs=in_specs,
        out_specs=out_specs,
        out_shape=out_shape,
        scratch_shapes=scratch,
        input_output_aliases=aliases,
        compiler_params=pltpu.CompilerParams(dimension_semantics=("arbitrary",), vmem_limit_bytes=VMEM_LIMIT),
        name="mixer_latent" if latent else "mixer_context",
    )(*args)


def _ffn_kernel(x_ref, mod_ref, w1_ref, b1_ref, w2_ref, b2_ref, g_ref, b_ref, o_ref, *, layer):
    lrow = slice(layer, layer + 1)
    mod = mod_ref[...]
    sh2, sc2, g2 = mod[:, 3 * D_MODEL:4 * D_MODEL], mod[:, 4 * D_MODEL:5 * D_MODEL], mod[:, 5 * D_MODEL:]
    for r in range(x_ref.shape[0] // FF_ROWS):
        rows = slice(r * FF_ROWS, (r + 1) * FF_ROWS)
        x = x_ref[rows, :]
        h = (x * (1.0 + sc2) + sh2).astype(BF16)
        acc = None
        for c in range(D_FF // FF_CHUNK):
            cols = slice(c * FF_CHUNK, (c + 1) * FF_CHUNK)
            t = jnp.dot(h, w1_ref[:, cols], preferred_element_type=F32) + b1_ref[lrow, cols]
            t = jnp.square(jnp.maximum(t, 0.0)).astype(BF16)
            o = jnp.dot(t, w2_ref[cols, :], preferred_element_type=F32)
            acc = o if acc is None else acc + o
        f = acc + b2_ref[lrow, :]
        o_ref[rows, :] = _layernorm(ALPHA * x + g2 * f, g_ref[lrow, :], b_ref[lrow, :])


def _ffn(x2, mod, p, layer, *, seq_len, latent, tm):
    rows = x2.shape[0]
    mod_row = (lambda i: (i * tm) // seq_len) if latent else (lambda i: mod.shape[1] - 1)
    in_specs = [pl.BlockSpec((tm, D_MODEL), lambda i: (i, 0)),
                pl.BlockSpec((None, None, 1, 6 * D_MODEL), lambda i: (layer, mod_row(i), 0, 0))]
    args = [x2, mod]
    for name in ("w_ff1", "b_ff1", "w_ff2", "b_ff2", "ln2_g", "ln2_b"):
        a = p[name]
        in_specs.append(_const_spec(a) if a.ndim == 2 else _layer_spec(a, layer))
        args.append(a)
    return pl.pallas_call(
        functools.partial(_ffn_kernel, layer=layer),
        grid=(rows // tm,),
        in_specs=in_specs,
        out_specs=pl.BlockSpec((tm, D_MODEL), lambda i: (i, 0)),
        out_shape=jax.ShapeDtypeStruct((rows, D_MODEL), F32),
        compiler_params=pltpu.CompilerParams(dimension_semantics=("arbitrary",), vmem_limit_bytes=VMEM_LIMIT),
        name="ffn_latent" if latent else "ffn_context",
    )(*args)


def _rope_tables(length):
    rows = length // GRID_W
    pos_row = np.repeat(np.arange(rows), GRID_W).astype(np.float32)
    pos_col = np.tile(np.arange(GRID_W), rows).astype(np.float32)
    n_f = HEAD_DIM // 4
    inv = (ROPE_THETA ** (-np.arange(n_f, dtype=np.float32) / n_f)).astype(np.float32)
    ang = np.concatenate([pos_row[:, None] * inv, pos_col[:, None] * inv], -1)
    cos = np.repeat(np.cos(ang), 2, axis=-1)
    sin = np.repeat(np.sin(ang), 2, axis=-1)
    even = (np.arange(HEAD_DIM) % 2 == 0)[None, :]
    sin_a = np.where(even, -sin, 0.0)
    sin_b = np.where(even, 0.0, sin)
    rep = LANES // HEAD_DIM
    return tuple(jnp.asarray(np.tile(t, (1, rep)), F32) for t in (cos, sin_a, sin_b))


def _block_diag(w):
    eye = jnp.eye(LRU_BLOCKS, dtype=w.dtype)
    full = w[..., :, :, None, :] * eye[:, None, :, None]
    return full.reshape(w.shape[:-3] + (LRU_W, LRU_W))


def kernel(x_prompt, x_sample, c, cache_k, cache_v, state_lru, c_ctx, w_ada, b_ada, w_in, q_norm_g, k_norm_g,
           conv_w, conv_b, lru_wa, lru_ba, lru_wx, lru_bx, lru_lam, mlp_norm_g, mlp_norm_b, mlp_ws, mlp_bs,
           w_out, ln1_g, ln1_b, w_ff1, b_ff1, w_ff2, b_ff2, ln2_g, ln2_b):
    batch, seq, _ = x_prompt.shape
    dec_batch, dec_seq, _ = x_sample.shape
    past = cache_k.shape[2]

    n_cond = dec_batch + 1
    cond_rows = -(-n_cond // SUBLANES) * SUBLANES
    cond = jnp.concatenate([c, c_ctx[None, :], jnp.zeros((cond_rows - n_cond, D_MODEL), F32)], axis=0)
    mod = _modulation(cond, w_ada, b_ada)[:, :n_cond].reshape(DEPTH, n_cond, 1, 6 * D_MODEL)

    head_id = jnp.arange(QK_W) // HEAD_DIM
    wg = jnp.stack([_block_diag(lru_wa), _block_diag(lru_wx)], axis=2)
    wg = wg.transpose(0, 3, 1, 2, 4).reshape(DEPTH, LRU_W, 4 * LRU_W)
    p = dict(
        w_in=w_in.astype(BF16),
        hsum=(head_id[:, None] == head_id[None, :]).astype(BF16),
        gqk=jnp.concatenate([jnp.tile(q_norm_g * Q_SCALE, (1, N_HEADS)), jnp.tile(k_norm_g, (1, N_KV_HEADS))], axis=1),
        conv_w=conv_w, conv_b=conv_b,
        wg=wg.astype(BF16),
        bg=jnp.stack([lru_ba, lru_bx], axis=2).reshape(DEPTH, 4 * LRU_W),
        lam=lru_lam,
        mlp_g=mlp_norm_g, mlp_b=mlp_norm_b,
        ws=mlp_ws.reshape(DEPTH, MLP_GROUPS * CHUNK, CHUNK).astype(BF16),
        bsb=jnp.repeat(mlp_bs.transpose(0, 2, 1), MLP_GW, axis=2),
        w_out=w_out.astype(BF16), ln1_g=ln1_g, ln1_b=ln1_b,
        w_ff1=w_ff1.astype(BF16), b_ff1=b_ff1, w_ff2=w_ff2.astype(BF16), b_ff2=b_ff2,
        ln2_g=ln2_g, ln2_b=ln2_b,
    )
    rope = _rope_tables(dec_seq)
    ctx = (cache_k.reshape(dec_batch, DEPTH, past, KV_W), cache_v.reshape(dec_batch, DEPTH, past, KV_W), state_lru)

    y_prompt = x_prompt.reshape(batch * seq, D_MODEL)
    y_sample = x_sample.reshape(dec_batch * dec_seq, D_MODEL)
    caches = None
    for l in range(DEPTH):
        x1, *caches = _mixer(y_prompt, mod, p, l, batch=batch, seq_len=seq, n_seq=2, latent=False, caches=caches)
        y_prompt = _ffn(x1, mod, p, l, seq_len=seq, latent=False, tm=FF_TILE)
        (x1,) = _mixer(y_sample, mod, p, l, batch=dec_batch, seq_len=dec_seq, n_seq=1, latent=True,
                       ctx=ctx, rope=rope)
        y_sample = _ffn(x1, mod, p, l, seq_len=dec_seq, latent=True, tm=FF_TILE)

    new_k, new_v, new_s = caches
    return (y_prompt.reshape(batch, seq, D_MODEL), y_sample.reshape(dec_batch, dec_seq, D_MODEL),
            new_k.reshape(batch, DEPTH, seq, N_KV_HEADS, HEAD_DIM),
            new_v.reshape(batch, DEPTH, seq, N_KV_HEADS, HEAD_DIM), new_s)
```

```python
import functools
import math

import jax
import jax.numpy as jnp
import numpy as np
from jax import lax
from jax.experimental import pallas as pl
from jax.experimental.pallas import tpu as pltpu

D_MODEL = 1024
DEPTH = 2
GRID_W = 64
CHUNK = 128
HEAD_DIM = 64
ATTN_W = 512
N_HEADS = 8
N_KV_HEADS = 2
KV_W = 128
LRU_W = 256
LRU_BLOCKS = 4
LRU_BW = 64
CONV_W = 4
RG_C = 8.0
MLP_W = 256
MLP_GROUPS = 4
MLP_GW = 64
MIX_W = 1024
IN_W = 1792
D_FF = 4096
ROPE_THETA = 10000.0
ALPHA = (2 * DEPTH) ** 0.25
EPS = 1e-6

Q0, K0, V0, XR0, GR0, ZM0 = 0, 512, 640, 768, 1024, 1280
QK_W = ATTN_W + KV_W
LANES = 128
SUBLANES = 8
N_QGROUPS = ATTN_W // LANES
HEADS_PER_KV = N_HEADS // N_KV_HEADS
Q_BLOCK = 256
OUT_ROWS = 256
MIN_OUT_CHUNKS = 4
FF_CHUNK = 1024
FF_TILE = 1024
FF_ROWS = 512
VMEM_LIMIT = 60 * 1024 * 1024
Q_SCALE = HEAD_DIM ** -0.5 * math.log2(math.e)

F32 = jnp.float32
BF16 = jnp.bfloat16


def _layernorm(x, g, b):
    mu = jnp.mean(x, -1, keepdims=True)
    xc = x - mu
    var = jnp.mean(xc * xc, -1, keepdims=True)
    return xc * lax.rsqrt(var + EPS) * g + b


def _gelu(x):
    k = -2.0 * math.sqrt(2.0 / math.pi) * math.log2(math.e)
    return x / (1.0 + jnp.exp2(x * (k + (k * 0.044715) * (x * x))))


def _layer_spec(arr, layer):
    n = arr.ndim - 1
    return pl.BlockSpec((None,) + arr.shape[1:], lambda *_: (layer,) + (0,) * n, pipeline_mode=pl.Buffered(1))


def _const_spec(arr):
    n = arr.ndim
    return pl.BlockSpec(arr.shape, lambda *_: (0,) * n, pipeline_mode=pl.Buffered(1))


def _mod_kernel(cond_ref, w_ref, b_ref, o_ref):
    cond = cond_ref[...]
    o_ref[...] = jnp.dot(jax.nn.silu(cond).astype(BF16), w_ref[...].astype(BF16),
                         preferred_element_type=F32) + b_ref[...]


def _modulation(cond, w_ada, b_ada):
    rows = cond.shape[0]
    tn = 1536
    return pl.pallas_call(
        _mod_kernel,
        grid=(DEPTH, 6 * D_MODEL // tn),
        in_specs=[pl.BlockSpec((rows, D_MODEL), lambda l, j: (0, 0)),
                  pl.BlockSpec((None, D_MODEL, tn), lambda l, j: (l, 0, j)),
                  pl.BlockSpec((None, 1, tn), lambda l, j: (l, 0, j))],
        out_specs=pl.BlockSpec((None, rows, tn), lambda l, j: (l, 0, j)),
        out_shape=jax.ShapeDtypeStruct((DEPTH, rows, 6 * D_MODEL), F32),
        compiler_params=pltpu.CompilerParams(dimension_semantics=("arbitrary", "arbitrary"),
                                             vmem_limit_bytes=VMEM_LIMIT),
        name="adaln_modulation",
    )(cond, w_ada, b_ada.reshape(DEPTH, 1, 6 * D_MODEL))


def _tile_scan(a, u, reverse):
    n_t, w = a.shape[0] // SUBLANES, a.shape[1]
    a3 = a.reshape(n_t, SUBLANES, w)
    u3 = u.reshape(n_t, SUBLANES, w)
    sub = lax.broadcasted_iota(jnp.int32, (1, SUBLANES, w), 1)
    d = 1
    while d < SUBLANES:
        keep = (sub < SUBLANES - d) if reverse else (sub >= d)
        shift = SUBLANES - d if reverse else d
        a_sh = pltpu.roll(a3, shift, 1)
        u_sh = pltpu.roll(u3, shift, 1)
        u3 = a3 * jnp.where(keep, u_sh, 0.0) + u3
        a3 = a3 * jnp.where(keep, a_sh, 1.0)
        d *= 2
    return a3.reshape(n_t * SUBLANES, w), u3.reshape(n_t * SUBLANES, w)


def _mixer_kernel(*refs, seq_len, n_seq, latent, layer):
    it = iter(refs)
    lrow = slice(layer, layer + 1)
    x_ref, mod_ref, w_in_ref, hsum_ref, gqk_ref = next(it), next(it), next(it), next(it), next(it)
    if latent:
        cos_ref, sin_a_ref, sin_b_ref = next(it), next(it), next(it)
        ck_ref, cv_ref, st_ref = next(it), next(it), next(it)
    conv_w_ref, conv_b_ref, wg_ref, bg_ref, lam_ref = next(it), next(it), next(it), next(it), next(it)
    mlp_g_ref, mlp_b_ref, ws_ref, bsb_ref = next(it), next(it), next(it), next(it)
    w_out_ref, ln_g_ref, ln_b_ref = next(it), next(it), next(it)
    if not latent:
        next(it), next(it), next(it)
    x1_ref = next(it)
    if not latent:
        k_out_ref, v_out_ref, s_out_ref = next(it), next(it), next(it)
    z_s, q_s, kt_s, vp_s, o_s, mix_s = next(it), next(it), next(it), next(it), next(it), next(it)
    e_bufs, il_bufs = (next(it), next(it)), (next(it), next(it))
    af_s, uf_s, ab_s, ub_s = next(it), next(it), next(it), next(it)

    L = seq_len
    past = ck_ref.shape[0] if latent else 0
    T = past + L
    n_tiles = L // SUBLANES
    kv_w = HEADS_PER_KV * HEAD_DIM

    mod = mod_ref[...]
    sh1, sc1, g1 = mod[:, 0:D_MODEL], mod[:, D_MODEL:2 * D_MODEL], mod[:, 2 * D_MODEL:3 * D_MODEL]
    h = (x_ref[...] * (1.0 + sc1) + sh1).astype(BF16)
    z_s[...] = jnp.dot(h, w_in_ref[...], preferred_element_type=F32)

    qk = z_s[:, Q0:Q0 + QK_W]
    ssq = jnp.dot((qk * qk).astype(BF16), hsum_ref[...], preferred_element_type=F32)
    qkn = qk * lax.rsqrt(ssq * (1.0 / HEAD_DIM) + EPS) * gqk_ref[lrow, :]

    lam = lam_ref[...]
    neg = -lam
    decay = -RG_C * (jnp.maximum(neg, 0.0) + jnp.log1p(jnp.exp(-jnp.abs(neg))))
    quarter = lax.broadcasted_iota(jnp.int32, (1, kv_w), 1) // HEAD_DIM

    for s in range(n_seq):
        r0 = s * L
        rows = slice(r0, r0 + L)
        qkn_s = qkn[rows]
        k_cur = qkn_s[:, ATTN_W:]
        v_cur = z_s[rows, V0:V0 + KV_W]
        if latent:
            cos, sin_a, sin_b = cos_ref[...], sin_a_ref[...], sin_b_ref[...]

            def rope(t):
                return (t * cos + pltpu.roll(t, LANES - 1, 1) * sin_a + pltpu.roll(t, 1, 1) * sin_b)

            for g in range(N_QGROUPS):
                q_s[g, rows, :] = rope(qkn_s[:, g * LANES:(g + 1) * LANES]).astype(BF16)
            k_all = jnp.concatenate([ck_ref[...], rope(k_cur)], axis=0)
            v_all = jnp.concatenate([cv_ref[...], v_cur], axis=0)
        else:
            k_out_ref[s] = k_cur
            v_out_ref[s] = v_cur
            for g in range(N_QGROUPS):
                q_s[g, rows, :] = qkn_s[:, g * LANES:(g + 1) * LANES].astype(BF16)
            k_all, v_all = k_cur, v_cur

        kt = k_all.T.astype(BF16)
        zer = jnp.zeros((HEAD_DIM, T), BF16)
        for j in range(N_KV_HEADS):
            kj = kt[j * HEAD_DIM:(j + 1) * HEAD_DIM]
            kt_s[s, 2 * j] = jnp.concatenate([kj, zer], axis=0)
            kt_s[s, 2 * j + 1] = jnp.concatenate([zer, kj], axis=0)
        v_rot = pltpu.roll(v_all, HEAD_DIM, 1)
        v_two = jnp.concatenate([v_all, v_all], axis=1).astype(BF16)
        v_rot_two = jnp.concatenate([v_rot, v_rot], axis=1).astype(BF16)
        zero_v = jnp.zeros((), BF16)
        for j in range(N_KV_HEADS):
            for hq in range(HEADS_PER_KV):
                src = v_two if hq % 2 == j else v_rot_two
                vp_s[s, j, hq] = jnp.where(quarter == hq, src, zero_v)

    qb = min(Q_BLOCK, L)
    n_qb = L // qb
    n_units = n_seq * N_KV_HEADS * n_qb

    def unit_index(n):
        s = n // (N_KV_HEADS * n_qb)
        j = (n // n_qb) % N_KV_HEADS
        qrows = pl.ds(pl.multiple_of(s * L + (n % n_qb) * qb, qb), qb)
        return s, j, qrows

    def score_stage(n, e_ref, il_ref):
        s, j, qrows = unit_index(n)
        inv = None
        for hq in range(HEADS_PER_KV):
            sc = jnp.dot(q_s[2 * j + hq // 2, qrows, :], kt_s[s, 2 * j + hq % 2],
                         preferred_element_type=F32)
            e = jnp.exp2(sc - jnp.max(sc, -1, keepdims=True))
            e_ref[hq] = e.astype(BF16)
            il = 1.0 / jnp.sum(e, -1, keepdims=True)
            inv = il if inv is None else jnp.where(quarter >= hq, il, inv)
        il_ref[...] = inv

    def value_stage(n, e_ref, il_ref):
        s, j, qrows = unit_index(n)
        acc = None
        for hq in range(HEADS_PER_KV):
            o = jnp.dot(e_ref[hq], vp_s[s, j, hq], preferred_element_type=F32)
            acc = o if acc is None else acc + o
        o_s[j, qrows, :] = (acc * il_ref[...]).astype(BF16)

    def unit_pair(k, carry):
        score_stage(2 * k + 1, e_bufs[1], il_bufs[1])
        value_stage(2 * k, e_bufs[0], il_bufs[0])
        score_stage(2 * k + 2, e_bufs[0], il_bufs[0])
        value_stage(2 * k + 1, e_bufs[1], il_bufs[1])
        return carry

    score_stage(0, e_bufs[0], il_bufs[0])
    lax.fori_loop(0, n_units // 2 - 1, unit_pair, 0)
    score_stage(n_units - 1, e_bufs[1], il_bufs[1])
    value_stage(n_units - 2, e_bufs[0], il_bufs[0])
    value_stage(n_units - 1, e_bufs[1], il_bufs[1])

    for s in range(n_seq):
        r0 = s * L
        rows = slice(r0, r0 + L)
        xr = z_s[rows, XR0:XR0 + LRU_W]
        row = lax.broadcasted_iota(jnp.int32, (L, LRU_W), 0)
        cw = conv_w_ref[...]
        xc = (jnp.where(row >= 1, pltpu.roll(xr, 1, 0), 0.0) * cw[0:1]
              + xr * cw[1:2]
              + jnp.where(row < L - 1, pltpu.roll(xr, L - 1, 0), 0.0) * cw[2:3]
              + jnp.where(row < L - 2, pltpu.roll(xr, L - 2, 0), 0.0) * cw[3:4]
              + conv_b_ref[lrow, :])
        xcb = xc.astype(BF16)
        xc2 = xc * math.sqrt(2.0)
        for d, (a_s, u_s) in enumerate(((af_s, uf_s), (ab_s, ub_s))):
            gates = jnp.dot(xcb, wg_ref[:, 2 * d * LRU_W:2 * (d + 1) * LRU_W],
                            preferred_element_type=F32) + bg_ref[lrow, 2 * d * LRU_W:2 * (d + 1) * LRU_W]
            r = jax.nn.sigmoid(gates[:, :LRU_W])
            i_g = jax.nn.sigmoid(gates[:, LRU_W:])
            log_a = decay[d:d + 1] * r
            a = jnp.exp(log_a)
            th = jnp.tanh(log_a)
            w = th / (th - 1.0)
            u = jnp.where(w > 0.0, w * lax.rsqrt(w), 0.0) * i_g * xc2
            a_t, u_t = _tile_scan(a, u, reverse=(d == 1))
            a_s[rows, :] = a_t
            u_s[rows, :] = u_t

        def tile_step(t, carry):
            cf, cb = carry
            rf = slice(r0 + t * SUBLANES, r0 + (t + 1) * SUBLANES)
            rb = slice(r0 + (n_tiles - 1 - t) * SUBLANES, r0 + (n_tiles - t) * SUBLANES)
            hf = af_s[rf, :] * cf + uf_s[rf, :]
            hb = ab_s[rb, :] * cb + ub_s[rb, :]
            uf_s[rf, :] = hf
            ub_s[rb, :] = hb
            return (jnp.broadcast_to(hf[SUBLANES - 1:SUBLANES], (SUBLANES, LRU_W)),
                    jnp.broadcast_to(hb[0:1], (SUBLANES, LRU_W)))

        if latent:
            carry = (jnp.broadcast_to(st_ref[0:1, :], (SUBLANES, LRU_W)),
                     jnp.broadcast_to(st_ref[1:2, :], (SUBLANES, LRU_W)))
        else:
            carry = (jnp.zeros((SUBLANES, LRU_W), F32), jnp.zeros((SUBLANES, LRU_W), F32))
        for t in range(n_tiles):
            carry = tile_step(t, carry)
        fin_f, fin_b = carry
        if not latent:
            s_out_ref[s, 0:1, :] = fin_f[0:1]
            s_out_ref[s, 1:2, :] = fin_b[0:1]
        y = (uf_s[rows, :] + ub_s[rows, :]) * _gelu(z_s[rows, GR0:GR0 + LRU_W])
        mix_s[rows, 0:LRU_W] = y.astype(BF16)

        glane = lax.broadcasted_iota(jnp.int32, (CHUNK, MLP_W), 1) // MLP_GW
        for c in range(L // CHUNK):
            crow = slice(r0 + c * CHUNK, r0 + (c + 1) * CHUNK)
            zg = _gelu(z_s[crow, ZM0:ZM0 + 2 * MLP_W])
            u_m = zg[:, :MLP_W]
            vn = _layernorm(zg[:, MLP_W:], mlp_g_ref[lrow, :], mlp_b_ref[lrow, :]).astype(BF16)
            full = jnp.dot(ws_ref[...], vn, preferred_element_type=F32)
            sm = full[(MLP_GROUPS - 1) * CHUNK:]
            for gi in range(MLP_GROUPS - 2, -1, -1):
                sm = jnp.where(glane == gi, full[gi * CHUNK:(gi + 1) * CHUNK], sm)
            mix_s[crow, LRU_W:] = (u_m * (sm + bsb_ref[...])).astype(BF16)

    m_rows = n_seq * L
    out_rows = OUT_ROWS if m_rows >= MIN_OUT_CHUNKS * OUT_ROWS else m_rows
    for c in range(m_rows // out_rows):
        rows = slice(c * out_rows, (c + 1) * out_rows)
        mixed = jnp.concatenate([o_s[j, rows, :] for j in range(N_KV_HEADS)] + [mix_s[rows, :]], axis=1)
        out = jnp.dot(mixed, w_out_ref[...], preferred_element_type=F32)
        x1_ref[rows, :] = _layernorm(ALPHA * x_ref[rows, :] + g1 * out, ln_g_ref[lrow, :], ln_b_ref[lrow, :])


def _mixer(x2, mod, p, layer, *, batch, seq_len, n_seq, latent, ctx=None, rope=None, caches=None):
    B, L = batch, seq_len
    M = n_seq * L
    n_steps = B // n_seq
    T = L + (ctx[0].shape[2] if latent else 0)

    mod_row = (lambda i: i) if latent else (lambda i: mod.shape[1] - 1)
    in_specs = [pl.BlockSpec((M, D_MODEL), lambda i: (i, 0)),
                pl.BlockSpec((None, None, 1, 6 * D_MODEL), lambda i: (layer, mod_row(i), 0, 0))]
    args = [x2, mod]
    for name in ("w_in", "hsum", "gqk"):
        a = p[name]
        in_specs.append(_layer_spec(a, layer) if name == "w_in" else _const_spec(a))
        args.append(a)
    if latent:
        for a in rope:
            in_specs.append(_const_spec(a))
            args.append(a)
        ck, cv, st = ctx
        past = ck.shape[2]
        in_specs += [pl.BlockSpec((None, None, past, KV_W), lambda i: (i, layer, 0, 0)),
                     pl.BlockSpec((None, None, past, KV_W), lambda i: (i, layer, 0, 0)),
                     pl.BlockSpec((None, None, 2, LRU_W), lambda i: (i, layer, 0, 0))]
        args += [ck, cv, st]
    for name in ("conv_w", "conv_b", "wg", "bg", "lam", "mlp_g", "mlp_b", "ws", "bsb", "w_out", "ln1_g", "ln1_b"):
        a = p[name]
        in_specs.append(_const_spec(a) if a.ndim == 2 else _layer_spec(a, layer))
        args.append(a)

    out_shape = [jax.ShapeDtypeStruct((B * L, D_MODEL), F32)]
    out_specs = [pl.BlockSpec((M, D_MODEL), lambda i: (i, 0))]
    aliases = {}
    if not latent:
        if caches is None:
            caches = (jnp.zeros((B, DEPTH, L, KV_W), F32), jnp.zeros((B, DEPTH, L, KV_W), F32),
                      jnp.zeros((B, DEPTH, 2, LRU_W), F32))
        for c in caches:
            aliases[len(args)] = len(out_shape)
            in_specs.append(pl.BlockSpec(memory_space=pl.ANY))
            args.append(c)
            out_shape.append(jax.ShapeDtypeStruct(c.shape, F32))
        out_specs += [pl.BlockSpec((n_seq, None, L, KV_W), lambda i: (i, layer, 0, 0)),
                      pl.BlockSpec((n_seq, None, L, KV_W), lambda i: (i, layer, 0, 0)),
                      pl.BlockSpec((n_seq, None, 2, LRU_W), lambda i: (i, layer, 0, 0))]

    qb = min(Q_BLOCK, L)
    kv_w = HEADS_PER_KV * HEAD_DIM
    scratch = [pltpu.VMEM((M, IN_W), F32),
               pltpu.VMEM((N_QGROUPS, M, LANES), BF16),
               pltpu.VMEM((n_seq, 2 * N_KV_HEADS, LANES, T), BF16),
               pltpu.VMEM((n_seq, N_KV_HEADS, HEADS_PER_KV, T, kv_w), BF16),
               pltpu.VMEM((N_KV_HEADS, M, kv_w), BF16),
               pltpu.VMEM((M, LRU_W + MLP_W), BF16)]
    scratch += [pltpu.VMEM((HEADS_PER_KV, qb, T), BF16)] * 2
    scratch += [pltpu.VMEM((qb, kv_w), F32)] * 2
    scratch += [pltpu.VMEM((M, LRU_W), F32)] * 4

    return pl.pallas_call(
        functools.partial(_mixer_kernel, seq_len=L, n_seq=n_seq, latent=latent, layer=layer),
        grid=(n_steps,),
        in_specs=in_specs,
        out_specs=out_specs,
        out_shape=out_shape,
        scratch_shapes=scratch,
        input_output_aliases=aliases,
        compiler_params=pltpu.CompilerParams(dimension_semantics=("arbitrary",), vmem_limit_bytes=VMEM_LIMIT),
        name="mixer_latent" if latent else "mixer_context",
    )(*args)


def _ffn_kernel(x_ref, mod_ref, w1_ref, b1_ref, w2_ref, b2_ref, g_ref, b_ref, o_ref, *, layer):
    lrow = slice(layer, layer + 1)
    mod = mod_ref[...]
    sh2, sc2, g2 = mod[:, 3 * D_MODEL:4 * D_MODEL], mod[:, 4 * D_MODEL:5 * D_MODEL], mod[:, 5 * D_MODEL:]
    for r in range(x_ref.shape[0] // FF_ROWS):
        rows = slice(r * FF_ROWS, (r + 1) * FF_ROWS)
        x = x_ref[rows, :]
        h = (x * (1.0 + sc2) + sh2).astype(BF16)
        acc = None
        for c in range(D_FF // FF_CHUNK):
            cols = slice(c * FF_CHUNK, (c + 1) * FF_CHUNK)
            t = jnp.dot(h, w1_ref[:, cols], preferred_element_type=F32) + b1_ref[lrow, cols]
            t = jnp.square(jnp.maximum(t, 0.0)).astype(BF16)
            o = jnp.dot(t, w2_ref[cols, :], preferred_element_type=F32)
            acc = o if acc is None else acc + o
        f = acc + b2_ref[lrow, :]
        o_ref[rows, :] = _layernorm(ALPHA * x + g2 * f, g_ref[lrow, :], b_ref[lrow, :])


def _ffn(x2, mod, p, layer, *, seq_len, latent, tm):
    rows = x2.shape[0]
    mod_row = (lambda i: (i * tm) // seq_len) if latent else (lambda i: mod.shape[1] - 1)
    in_specs = [pl.BlockSpec((tm, D_MODEL), lambda i: (i, 0)),
                pl.BlockSpec((None, None, 1, 6 * D_MODEL), lambda i: (layer, mod_row(i), 0, 0))]
    args = [x2, mod]
    for name in ("w_ff1", "b_ff1", "w_ff2", "b_ff2", "ln2_g", "ln2_b"):
        a = p[name]
        in_specs.append(_const_spec(a) if a.ndim == 2 else _layer_spec(a, layer))
        args.append(a)
    return pl.pallas_call(
        functools.partial(_ffn_kernel, layer=layer),
        grid=(rows // tm,),
        in_specs=in_specs,
        out_specs=pl.BlockSpec((tm, D_MODEL), lambda i: (i, 0)),
        out_shape=jax.ShapeDtypeStruct((rows, D_MODEL), F32),
        compiler_params=pltpu.CompilerParams(dimension_semantics=("arbitrary",), vmem_limit_bytes=VMEM_LIMIT),
        name="ffn_latent" if latent else "ffn_context",
    )(*args)


def _rope_tables(length):
    rows = length // GRID_W
    pos_row = np.repeat(np.arange(rows), GRID_W).astype(np.float32)
    pos_col = np.tile(np.arange(GRID_W), rows).astype(np.float32)
    n_f = HEAD_DIM // 4
    inv = (ROPE_THETA ** (-np.arange(n_f, dtype=np.float32) / n_f)).astype(np.float32)
    ang = np.concatenate([pos_row[:, None] * inv, pos_col[:, None] * inv], -1)
    cos = np.repeat(np.cos(ang), 2, axis=-1)
    sin = np.repeat(np.sin(ang), 2, axis=-1)
    even = (np.arange(HEAD_DIM) % 2 == 0)[None, :]
    sin_a = np.where(even, -sin, 0.0)
    sin_b = np.where(even, 0.0, sin)
    rep = LANES // HEAD_DIM
    return tuple(jnp.asarray(np.tile(t, (1, rep)), F32) for t in (cos, sin_a, sin_b))


def _block_diag(w):
    eye = jnp.eye(LRU_BLOCKS, dtype=w.dtype)
    full = w[..., :, :, None, :] * eye[:, None, :, None]
    return full.reshape(w.shape[:-3] + (LRU_W, LRU_W))


def kernel(x_prompt, x_sample, c, cache_k, cache_v, state_lru, c_ctx, w_ada, b_ada, w_in, q_norm_g, k_norm_g,
           conv_w, conv_b, lru_wa, lru_ba, lru_wx, lru_bx, lru_lam, mlp_norm_g, mlp_norm_b, mlp_ws, mlp_bs,
           w_out, ln1_g, ln1_b, w_ff1, b_ff1, w_ff2, b_ff2, ln2_g, ln2_b):
    batch, seq, _ = x_prompt.shape
    dec_batch, dec_seq, _ = x_sample.shape
    past = cache_k.shape[2]

    n_cond = dec_batch + 1
    cond_rows = -(-n_cond // SUBLANES) * SUBLANES
    cond = jnp.concatenate([c, c_ctx[None, :], jnp.zeros((cond_rows - n_cond, D_MODEL), F32)], axis=0)
    mod = _modulation(cond, w_ada, b_ada)[:, :n_cond].reshape(DEPTH, n_cond, 1, 6 * D_MODEL)

    head_id = jnp.arange(QK_W) // HEAD_DIM
    wg = jnp.stack([_block_diag(lru_wa), _block_diag(lru_wx)], axis=2)
    wg = wg.transpose(0, 3, 1, 2, 4).reshape(DEPTH, LRU_W, 4 * LRU_W)
    p = dict(
        w_in=w_in.astype(BF16),
        hsum=(head_id[:, None] == head_id[None, :]).astype(BF16),
        gqk=jnp.concatenate([jnp.tile(q_norm_g * Q_SCALE, (1, N_HEADS)), jnp.tile(k_norm_g, (1, N_KV_HEADS))], axis=1),
        conv_w=conv_w, conv_b=conv_b,
        wg=wg.astype(BF16),
        bg=jnp.stack([lru_ba, lru_bx], axis=2).reshape(DEPTH, 4 * LRU_W),
        lam=lru_lam,
        mlp_g=mlp_norm_g, mlp_b=mlp_norm_b,
        ws=mlp_ws.reshape(DEPTH, MLP_GROUPS * CHUNK, CHUNK).astype(BF16),
        bsb=jnp.repeat(mlp_bs.transpose(0, 2, 1), MLP_GW, axis=2),
        w_out=w_out.astype(BF16), ln1_g=ln1_g, ln1_b=ln1_b,
        w_ff1=w_ff1.astype(BF16), b_ff1=b_ff1, w_ff2=w_ff2.astype(BF16), b_ff2=b_ff2,
        ln2_g=ln2_g, ln2_b=ln2_b,
    )
    rope = _rope_tables(dec_seq)
    ctx = (cache_k.reshape(dec_batch, DEPTH, past, KV_W), cache_v.reshape(dec_batch, DEPTH, past, KV_W), state_lru)

    y_prompt = x_prompt.reshape(batch * seq, D_MODEL)
    y_sample = x_sample.reshape(dec_batch * dec_seq, D_MODEL)
    caches = None
    for l in range(DEPTH):
        x1, *caches = _mixer(y_prompt, mod, p, l, batch=batch, seq_len=seq, n_seq=2, latent=False, caches=caches)
        y_prompt = _ffn(x1, mod, p, l, seq_len=seq, latent=False, tm=FF_TILE)
        (x1,) = _mixer(y_sample, mod, p, l, batch=dec_batch, seq_len=dec_seq, n_seq=1, latent=True,
                       ctx=ctx, rope=rope)
        y_sample = _ffn(x1, mod, p, l, seq_len=dec_seq, latent=True, tm=FF_TILE)

    new_k, new_v, new_s = caches
    return (y_prompt.reshape(batch, seq, D_MODEL), y_sample.reshape(dec_batch, dec_seq, D_MODEL),
            new_k.reshape(batch, DEPTH, seq, N_KV_HEADS, HEAD_DIM),
            new_v.reshape(batch, DEPTH, seq, N_KV_HEADS, HEAD_DIM), new_s)
```
